```python
import jax, jax.numpy as jnp
from jax import lax
import numpy as np


D_MODEL = 1024
BATCH = 16
SEQ = 4096
DEPTH = 4

HEAD_DIM = 64
ROPE_THETA = 500000.0
ROT_DIM = HEAD_DIM // 4
NORM_EPS = 1e-6
BLOCK = 128

DSA_GROUPS = ((128, 1), (512, 4), (2048, 16))
DSA_HEADS_PER_GROUP = 4
DSA_HEADS = DSA_HEADS_PER_GROUP * len(DSA_GROUPS)
DSA_W = DSA_HEADS * HEAD_DIM
DSA_OUT = DSA_HEADS_PER_GROUP * HEAD_DIM

MLA_HEADS = 8
MLA_Q_LORA = 256
MLA_KV_LORA = 128
MLA_NOPE = 64
MLA_ROPE = 32
MLA_QK = MLA_NOPE + MLA_ROPE
MLA_V = 64
MLA_OUT = MLA_HEADS * MLA_V

SB_HEADS = 8
SB_W = SB_HEADS * HEAD_DIM
SB_OUT = SB_W

N_BRANCH = 3
D_FF = 4 * D_MODEL
BRANCH_IN = DSA_OUT + MLA_OUT + SB_OUT

_IN_SIZES = (DSA_W, DSA_W, DSA_W, MLA_Q_LORA, MLA_KV_LORA, MLA_ROPE, SB_W, SB_W, SB_W, N_BRANCH * D_MODEL)
IN_COLS = sum(_IN_SIZES)
IN_SPLITS = tuple(sum(_IN_SIZES[:i + 1]) for i in range(len(_IN_SIZES) - 1))

kernel_name = "hybrid_gated_dilated_mla_stickbreaking"


def _rms_norm(x, g):
    x32 = x.astype(jnp.float32)
    y = x32 * lax.rsqrt(jnp.mean(x32 * x32, axis=-1, keepdims=True) + NORM_EPS)
    return (y * g.astype(jnp.float32)).astype(x.dtype)


def _rope_tables(seq, dim):
    pos = jnp.arange(seq, dtype=jnp.float32)
    inv_freq = ROPE_THETA ** (-jnp.arange(0, dim, 2, dtype=jnp.float32) / dim)
    ang = pos[:, None] * inv_freq[None, :]
    return jnp.cos(ang), jnp.sin(ang)


def _apply_rope(x, cos, sin):
    half = x.shape[-1] // 2
    bshape = (cos.shape[0],) + (1,) * (x.ndim - 3) + (half,)
    c = cos.reshape(bshape)
    s = sin.reshape(bshape)
    x32 = x.astype(jnp.float32)
    x1, x2 = x32[..., :half], x32[..., half:]
    return jnp.concatenate([x1 * c - x2 * s, x2 * c + x1 * s], axis=-1).astype(x.dtype)


def _partial_rope(x, cos, sin):
    return jnp.concatenate([_apply_rope(x[..., :ROT_DIM], cos, sin), x[..., ROT_DIM:]], axis=-1)


def _strided_band_attention(q, k, v, window, dilation):
    B, S, H, Dh = q.shape
    span = window // dilation
    blk = span
    L = S // dilation
    nb = -(-L // blk)
    Lp = nb * blk

    def to_blocks(t):
        t = t.reshape(B, L, dilation, H, Dh).transpose(0, 2, 1, 3, 4)
        t = jnp.pad(t, ((0, 0), (0, 0), (0, Lp - L), (0, 0), (0, 0)))
        return t.reshape(B, dilation, nb, blk, H, Dh)

    def band(t):
        prev = jnp.pad(t, ((0, 0), (0, 0), (1, 0), (0, 0), (0, 0), (0, 0)))[:, :, :-1]
        return jnp.concatenate([prev, t], axis=3)

    qb = to_blocks(q)
    kband = band(to_blocks(k))
    vband = band(to_blocks(v))
    s = jnp.einsum('brnqhd,brnkhd->brnhqk', qb, kband,
                   preferred_element_type=jnp.float32) * (Dh ** -0.5)
    qi = jnp.arange(blk)[:, None]
    kj = jnp.arange(2 * blk)[None, :]
    dist = qi + blk - kj
    first = (jnp.arange(nb) == 0)[:, None, None]
    valid = (dist >= 0)[None] & (dist <= span)[None] & ~(first & (kj < blk)[None])
    s = jnp.where(valid[:, None], s, -jnp.inf)
    m = jnp.max(s, axis=-1, keepdims=True)
    p = jnp.exp(s - m)
    den = jnp.sum(p, axis=-1, keepdims=True)
    o = jnp.einsum('brnhqk,brnkhd->brnqhd', (p / den).astype(v.dtype), vband)
    lse = (m + jnp.log(den))[..., 0]
    o = o.reshape(B, dilation, Lp, H, Dh)[:, :, :L].transpose(0, 2, 1, 3, 4).reshape(B, S, H, Dh)
    lse = lse.transpose(0, 1, 2, 4, 3).reshape(B, dilation, Lp, H)[:, :, :L]
    lse = lse.transpose(0, 2, 1, 3).reshape(B, S, H)
    return o, lse


def _dilated_mixture(q, k, v):
    outs, lses = [], []
    for g, (window, dil) in enumerate(DSA_GROUPS):
        o, lse = _strided_band_attention(q[:, :, g], k[:, :, g], v[:, :, g], window, dil)
        outs.append(o)
        lses.append(lse)
    wts = jax.nn.softmax(jnp.stack(lses, axis=0), axis=0)
    o = jnp.sum(wts[..., None] * jnp.stack(outs, axis=0).astype(jnp.float32), axis=0)
    return o.astype(q.dtype)


def _causal_softmax_attention(q, k, v):
    B, S, H, Dq = q.shape
    nb = S // BLOCK
    qb = q.reshape(B, nb, BLOCK, H, Dq).transpose(1, 0, 2, 3, 4)
    kpos = jnp.arange(S)
    scale = Dq ** -0.5

    def one(args):
        qi, start = args
        s = jnp.einsum('bqhd,bkhd->bhqk', qi, k, preferred_element_type=jnp.float32) * scale
        qpos = start + jnp.arange(BLOCK)
        s = jnp.where(kpos[None, :] <= qpos[:, None], s, -jnp.inf)
        p = jax.nn.softmax(s, axis=-1)
        return jnp.einsum('bhqk,bkhd->bqhd', p.astype(v.dtype), v)

    out = lax.map(one, (qb, jnp.arange(nb) * BLOCK))
    return out.transpose(1, 0, 2, 3, 4).reshape(B, S, H, v.shape[-1])


def _stick_breaking_attention(q, k, v):
    B, S, H, Dh = q.shape
    nb = S // BLOCK
    qb = q.reshape(B, nb, BLOCK, H, Dh).transpose(1, 0, 2, 3, 4)
    kpos = jnp.arange(S)
    scale = Dh ** -0.5

    def one(args):
        qi, start = args
        z = jnp.einsum('bqhd,bkhd->bhqk', qi, k, preferred_element_type=jnp.float32) * scale
        qpos = start + jnp.arange(BLOCK)
        strict = kpos[None, :] < qpos[:, None]
        log_beta = jax.nn.log_sigmoid(z)
        log_one_minus = jnp.where(strict, jax.nn.log_sigmoid(-z), 0.0)
        after = lax.cumsum(log_one_minus, axis=3, reverse=True) - log_one_minus
        weights = jnp.exp(jnp.where(strict, log_beta + after, -jnp.inf))
        return jnp.einsum('bhqk,bkhd->bqhd', weights.astype(v.dtype), v)

    out = lax.map(one, (qb, jnp.arange(nb) * BLOCK))
    return out.transpose(1, 0, 2, 3, 4).reshape(B, S, H, Dh)


def setup_inputs(seed: int = 0) -> dict:
    key = jax.random.key(seed)
    ks = jax.random.split(key, 20)
    f32 = jnp.float32

    def dense(k, shape, fan_in):
        return jax.random.normal(k, shape, f32) * (fan_in ** -0.5)

    def gain(k, n):
        return 1.0 + 0.02 * jax.random.normal(k, (DEPTH, n), f32)

    w_branch = jnp.concatenate([
        dense(ks[11], (DEPTH, DSA_OUT, D_MODEL), DSA_OUT),
        dense(ks[12], (DEPTH, MLA_OUT, D_MODEL), MLA_OUT),
        dense(ks[13], (DEPTH, SB_OUT, D_MODEL), SB_OUT)], axis=1)
    return {
        'x': jax.random.normal(ks[0], (BATCH, SEQ, D_MODEL), f32),
        'attn_norm': gain(ks[1], D_MODEL),
        'w_in': dense(ks[2], (DEPTH, D_MODEL, IN_COLS), D_MODEL),
        'a_q_norm': gain(ks[3], HEAD_DIM),
        'a_k_norm': gain(ks[4], HEAD_DIM),
        'b_q_a_norm': gain(ks[5], MLA_Q_LORA),
        'w_q_b': dense(ks[6], (DEPTH, MLA_Q_LORA, MLA_HEADS, MLA_QK), MLA_Q_LORA),
        'b_kv_a_norm': gain(ks[7], MLA_KV_LORA),
        'w_kv_b': dense(ks[8], (DEPTH, MLA_KV_LORA, MLA_HEADS, MLA_NOPE + MLA_V), MLA_KV_LORA),
        'b_q_norm': gain(ks[9], MLA_QK),
        'b_k_norm': gain(ks[10], MLA_QK),
        'w_branch': w_branch,
        'w_out': dense(ks[14], (DEPTH, D_MODEL, D_MODEL), D_MODEL),
        'mlp_norm': gain(ks[15], D_MODEL),
        'w_ff1': dense(ks[16], (DEPTH, D_MODEL, D_FF), D_MODEL),
        'w_ff2': dense(ks[17], (DEPTH, D_FF, D_MODEL), D_FF),
    }


def reference(x, attn_norm, w_in, a_q_norm, a_k_norm, b_q_a_norm, w_q_b, b_kv_a_norm,
              w_kv_b, b_q_norm, b_k_norm, w_branch, w_out, mlp_norm, w_ff1, w_ff2):
    B, S, _ = x.shape
    cos_p, sin_p = _rope_tables(S, ROT_DIM)
    cos_m, sin_m = _rope_tables(S, MLA_ROPE)
    grp = (B, S, len(DSA_GROUPS), DSA_HEADS_PER_GROUP, HEAD_DIM)
    for l in range(DEPTH):
        h = _rms_norm(x, attn_norm[l])
        proj = h @ w_in[l]
        a_q, a_k, a_v, b_ql, b_kvl, b_kr, c_q, c_k, c_v, gate = jnp.split(proj, IN_SPLITS, axis=-1)

        aq = _partial_rope(_rms_norm(a_q.reshape(grp), a_q_norm[l]), cos_p, sin_p)
        ak = _partial_rope(_rms_norm(a_k.reshape(grp), a_k_norm[l]), cos_p, sin_p)
        o_a = _dilated_mixture(aq, ak, a_v.reshape(grp)).reshape(B, S, DSA_OUT)

        bq = jnp.einsum('bsr,rhe->bshe', _rms_norm(b_ql, b_q_a_norm[l]), w_q_b[l])
        kv = jnp.einsum('bsr,rhe->bshe', _rms_norm(b_kvl, b_kv_a_norm[l]), w_kv_b[l])
        k_nope, bv = kv[..., :MLA_NOPE], kv[..., MLA_NOPE:]
        k_rope = jnp.broadcast_to(b_kr[:, :, None, :], (B, S, MLA_HEADS, MLA_ROPE))
        bk = jnp.concatenate([k_nope, k_rope], axis=-1)
        bq = _rms_norm(bq, b_q_norm[l])
        bk = _rms_norm(bk, b_k_norm[l])
        bq = jnp.concatenate([bq[..., :MLA_NOPE], _apply_rope(bq[..., MLA_NOPE:], cos_m, sin_m)], axis=-1)
        bk = jnp.concatenate([bk[..., :MLA_NOPE], _apply_rope(bk[..., MLA_NOPE:], cos_m, sin_m)], axis=-1)
        o_b = _causal_softmax_attention(bq, bk, bv).reshape(B, S, MLA_OUT)

        sb = (B, S, SB_HEADS, HEAD_DIM)
        o_c = _stick_breaking_attention(c_q.reshape(sb), c_k.reshape(sb), c_v.reshape(sb)).reshape(B, S, SB_OUT)

        gates = jax.nn.sigmoid(gate.reshape(B, S, N_BRANCH, D_MODEL))
        wb = w_branch[l]
        merged = (gates[:, :, 0] * (o_a @ wb[:DSA_OUT])
                  + gates[:, :, 1] * (o_b @ wb[DSA_OUT:DSA_OUT + MLA_OUT])
                  + gates[:, :, 2] * (o_c @ wb[DSA_OUT + MLA_OUT:]))
        x = x + merged @ w_out[l]

        h2 = _rms_norm(x, mlp_norm[l])
        x = x + jnp.square(jax.nn.relu(h2 @ w_ff1[l])) @ w_ff2[l]
    return x
```

```python
import functools

import jax
import jax.numpy as jnp
from jax import lax
from jax.experimental import pallas as pl
from jax.experimental.pallas import tpu as pltpu

D_MODEL = 1024
HEAD_DIM = 64
ROPE_THETA = 500000.0
ROT_DIM = HEAD_DIM // 4
NORM_EPS = 1e-6
BLOCK = 128

DSA_GROUPS = ((128, 1), (512, 4), (2048, 16))
DSA_HEADS_PER_GROUP = 4
DSA_W = DSA_HEADS_PER_GROUP * len(DSA_GROUPS) * HEAD_DIM
DSA_OUT = DSA_HEADS_PER_GROUP * HEAD_DIM

MLA_HEADS = 8
MLA_Q_LORA = 256
MLA_KV_LORA = 128
MLA_NOPE = 64
MLA_ROPE = 32
MLA_QK = MLA_NOPE + MLA_ROPE
MLA_V = 64
MLA_OUT = MLA_HEADS * MLA_V
MLA_PAD = 128

SB_HEADS = 8
SB_W = SB_HEADS * HEAD_DIM

N_BRANCH = 3
D_FF = 4 * D_MODEL

LANES = 128
MXU_DIM = 256
VMEM_LIMIT = 56 * 1024 * 1024

F32 = jnp.float32
BF16 = jnp.bfloat16
NEG = -1e30


def _dot(a, b):
    return jnp.dot(a, b, preferred_element_type=F32)


def _dot_nt(a, b):
    return lax.dot_general(a, b, (((1,), (1,)), ((), ())), preferred_element_type=F32)


def _split_dot(s, m):
    hi = s.astype(BF16)
    lo = (s - hi.astype(F32)).astype(BF16)
    return _dot(hi, m) + _dot(lo, m)


def _rms(x, g):
    ms = jnp.mean(x * x, axis=-1, keepdims=True)
    return x * lax.rsqrt(ms + NORM_EPS) * g


def _head_norm(y, m, inv_n, g):
    ss = _split_dot(y * y, m)
    return y * lax.rsqrt(ss * inv_n + NORM_EPS) * g


def _rope(y, c, s1, s2, shift):
    outs = []
    for j in range(y.shape[1] // LANES):
        yj = y[:, j * LANES:(j + 1) * LANES]
        outs.append(yj * c + pltpu.roll(yj, shift, 1) * s1 + pltpu.roll(yj, LANES - shift, 1) * s2)
    return jnp.concatenate(outs, axis=-1)


def _proj_kernel(x_ref, gn_ref, wa_ref, ws_ref, wc_ref, wqb_ref, wk_ref, wv_ref,
                 gaq_ref, gak_ref, gql_ref, gkvl_ref, gbq_ref, gbk_ref,
                 ca_ref, s1a_ref, s2a_ref, cm_ref, s1m_ref, s2m_ref, m64_ref, m128_ref,
                 aq_ref, ak_ref, av_ref, bq_ref, bk_ref, bv_ref, cq_ref, ck_ref, cv_ref):
    h = _rms(x_ref[...], gn_ref[...]).astype(BF16)
    ca, s1a, s2a = ca_ref[...], s1a_ref[...], s2a_ref[...]
    cm, s1m, s2m = cm_ref[...], s1m_ref[...], s2m_ref[...]
    m64, m128 = m64_ref[...], m128_ref[...]
    W = MXU_DIM

    for dst, g_ref, off in ((aq_ref, gaq_ref, 0), (ak_ref, gak_ref, DSA_W)):
        for c in range(DSA_W // W):
            y = _dot(h, wa_ref[:, off + c * W: off + (c + 1) * W])
            y = _head_norm(y, m64, 1.0 / HEAD_DIM, g_ref[:, c * W:(c + 1) * W])
            dst[:, c * W:(c + 1) * W] = _rope(y, ca, s1a, s2a, ROT_DIM // 2).astype(BF16)
    av_ref[...] = _dot(h, wa_ref[:, 2 * DSA_W:3 * DSA_W]).astype(BF16)

    ys = _dot(h, ws_ref[...])
    ql = _rms(ys[:, :MLA_Q_LORA], gql_ref[...]).astype(BF16)
    kvl = _rms(ys[:, MLA_Q_LORA:MLA_Q_LORA + MLA_KV_LORA], gkvl_ref[...]).astype(BF16)
    kr = ys[:, MLA_Q_LORA + MLA_KV_LORA:]
    kr2 = jnp.concatenate([kr, kr], axis=-1)
    for c in range(MLA_HEADS * MLA_PAD // W):
        yq = _dot(ql, wqb_ref[:, c * W:(c + 1) * W])
        yq = _head_norm(yq, m128, 1.0 / MLA_QK, gbq_ref[:, c * W:(c + 1) * W])
        bq_ref[:, c * W:(c + 1) * W] = _rope(yq, cm, s1m, s2m, MLA_ROPE // 2).astype(BF16)
        yk = _dot(kvl, wk_ref[:, c * W:(c + 1) * W]) + kr2
        yk = _head_norm(yk, m128, 1.0 / MLA_QK, gbk_ref[:, c * W:(c + 1) * W])
        bk_ref[:, c * W:(c + 1) * W] = _rope(yk, cm, s1m, s2m, MLA_ROPE // 2).astype(BF16)
    bv_ref[...] = _dot(kvl, wv_ref[...]).astype(BF16)

    cq_ref[...] = _dot(h, wc_ref[:, :SB_W]).astype(BF16)
    ck_ref[...] = _dot(h, wc_ref[:, SB_W:2 * SB_W]).astype(BF16)
    cv_ref[...] = _dot(h, wc_ref[:, 2 * SB_W:]).astype(BF16)


def _const_spec(shape):
    nd = len(shape)
    return pl.BlockSpec(shape, lambda *_: (0,) * nd)


def _proj(x, p, tabs, S, tm):
    T = x.shape[0]
    ns = S // tm
    row = lambda w: pl.BlockSpec((tm, w), lambda i: (i, 0))
    tab = pl.BlockSpec((tm, LANES), lambda i: (i % ns, 0))
    consts = [p['gn'], p['wa'], p['ws'], p['wc'], p['wqb'], p['wk'], p['wv'],
              p['gaq'], p['gak'], p['gql'], p['gkvl'], p['gbq'], p['gbk']]
    ins = [x] + consts + list(tabs['rope']) + [tabs['m64'], tabs['m128']]
    in_specs = ([row(D_MODEL)] + [_const_spec(c.shape) for c in consts] + [tab] * 6
                + [_const_spec((MXU_DIM, MXU_DIM))] * 2)
    widths = (DSA_W, DSA_W, DSA_W, MLA_HEADS * MLA_PAD, MLA_HEADS * MLA_PAD, MLA_OUT, SB_W, SB_W, SB_W)
    return pl.pallas_call(
        _proj_kernel,
        grid=(T // tm,),
        in_specs=in_specs,
        out_specs=[row(w) for w in widths],
        out_shape=[jax.ShapeDtypeStruct((T, w), BF16) for w in widths],
        compiler_params=pltpu.CompilerParams(vmem_limit_bytes=VMEM_LIMIT),
        name="proj",
    )(*ins)


def _dsa_kernel(q_ref, kp_ref, kc_ref, vp_ref, vc_ref, o_ref, lse_ref, *, nb_seq):
    n = pl.program_id(0)
    pen = jnp.where(n % nb_seq == 0, NEG, 0.0).astype(F32)
    qi = lax.broadcasted_iota(jnp.int32, (BLOCK, BLOCK), 0)
    kj = lax.broadcasted_iota(jnp.int32, (BLOCK, BLOCK), 1)
    for pr in range(DSA_OUT // LANES):
        sl = slice(pr * LANES, (pr + 1) * LANES)
        q, kp, kc, vp, vc = q_ref[:, sl], kp_ref[:, sl], kc_ref[:, sl], vp_ref[:, sl], vc_ref[:, sl]
        o_acc = jnp.zeros((BLOCK, LANES), F32)
        l_acc = jnp.zeros((BLOCK, LANES), F32)
        for hh in range(2):
            sel = (kj < HEAD_DIM) if hh == 0 else (kj >= HEAD_DIM)
            qh = jnp.where(sel, q, jnp.zeros_like(q))
            sp = jnp.where(kj >= qi, _dot_nt(qh, kp) + pen, NEG)
            sc = jnp.where(kj <= qi, _dot_nt(qh, kc), NEG)
            m = jnp.maximum(jnp.max(sp, axis=-1, keepdims=True), jnp.max(sc, axis=-1, keepdims=True))
            pp = jnp.exp(sp - m)
            pc = jnp.exp(sc - m)
            den = jnp.sum(pp, axis=-1, keepdims=True) + jnp.sum(pc, axis=-1, keepdims=True)
            o = (_dot(pp.astype(BF16), vp) + _dot(pc.astype(BF16), vc)) / den
            lse = m + jnp.log(den)
            o_acc = jnp.where(sel, o, o_acc)
            l_acc = jnp.where(sel, lse, l_acc)
        o_ref[:, sl] = o_acc
        lse_ref[:, sl] = l_acc


def _dsa(q, k, v, nb_seq):
    T = q.shape[0]
    cur = pl.BlockSpec((BLOCK, DSA_OUT), lambda n: (n, 0))
    prev = pl.BlockSpec((BLOCK, DSA_OUT), lambda n: (jnp.maximum(n - 1, 0), 0))
    return pl.pallas_call(
        functools.partial(_dsa_kernel, nb_seq=nb_seq),
        grid=(T // BLOCK,),
        in_specs=[cur, prev, cur, prev, cur],
        out_specs=[cur, cur],
        out_shape=[jax.ShapeDtypeStruct((T, DSA_OUT), F32)] * 2,
        name="dsa",
    )(q, k, k, v, v)


def _mla_kernel(q_ref, k_ref, v_ref, o_ref, *, tq):
    i = pl.program_id(2)
    row = lax.broadcasted_iota(jnp.int32, (tq, tq), 0)
    col = lax.broadcasted_iota(jnp.int32, (tq, tq), 1)
    lane = lax.broadcasted_iota(jnp.int32, (tq, LANES), 1)
    outs = []
    for hh in range(2):
        hs = slice(hh * MLA_PAD, (hh + 1) * MLA_PAD)
        q = q_ref[:, hs]

        def step(kb, carry, masked):
            m, l, acc = carry
            k = k_ref[pl.ds(kb * tq, tq), hs]
            v = v_ref[pl.ds(kb * tq, tq), :]
            s = _dot_nt(q, k)
            if masked:
                s = jnp.where(col <= row, s, NEG)
            m_new = jnp.maximum(m, jnp.max(s, axis=-1, keepdims=True))
            alpha = jnp.exp(m - m_new)
            p = jnp.exp(s - m_new)
            l = l * alpha + jnp.sum(p, axis=-1, keepdims=True)
            acc = acc * alpha + _dot(p.astype(BF16), v)
            return m_new, l, acc

        carry = (jnp.full((tq, 1), NEG, F32), jnp.zeros((tq, 1), F32), jnp.zeros((tq, LANES), F32))
        carry = lax.fori_loop(0, i, lambda kb, c: step(kb, c, False), carry)
        m, l, acc = step(i, carry, True)
        outs.append(acc / l)
    o_ref[...] = jnp.where(lane < MLA_V, outs[0], outs[1]).astype(BF16)


def _mla(q, k, v, B, S, tq):
    nq = S // tq
    npair = MLA_HEADS // 2
    return pl.pallas_call(
        functools.partial(_mla_kernel, tq=tq),
        grid=(B, npair, nq),
        in_specs=[pl.BlockSpec((tq, 2 * MLA_PAD), lambda b, j, i: (b * nq + i, j)),
                  pl.BlockSpec((S, 2 * MLA_PAD), lambda b, j, i: (b, j)),
                  pl.BlockSpec((S, LANES), lambda b, j, i: (b, j))],
        out_specs=pl.BlockSpec((tq, LANES), lambda b, j, i: (b * nq + i, j)),
        out_shape=jax.ShapeDtypeStruct((B * S, MLA_OUT), BF16),
        name="mla",
    )(q, k, v)


def _sb_kernel(q_ref, k_ref, v_ref, o_ref, *, tq):
    i = pl.program_id(2)
    row = lax.broadcasted_iota(jnp.int32, (tq, tq), 0)
    col = lax.broadcasted_iota(jnp.int32, (tq, tq), 1)
    lane = lax.broadcasted_iota(jnp.int32, (tq, LANES), 1)
    strict = col < row
    tri = jnp.where(row > col, 1.0, 0.0).astype(BF16)
    q = q_ref[...]
    outs = []
    for hh in range(2):
        qh = jnp.where((lane < HEAD_DIM) if hh == 0 else (lane >= HEAD_DIM), q, jnp.zeros_like(q))

        def step(kb, carry, masked):
            after, acc = carry
            k = k_ref[pl.ds(kb * tq, tq), :]
            v = v_ref[pl.ds(kb * tq, tq), :]
            z = _dot_nt(qh, k)
            log_beta = jnp.minimum(z, 0.0) - jnp.log(1.0 + jnp.exp(-jnp.abs(z)))
            lom = log_beta - z
            if masked:
                lom = jnp.where(strict, lom, 0.0)
            w = jnp.exp(log_beta + _split_dot(lom, tri) + after)
            if masked:
                w = jnp.where(strict, w, 0.0)
            acc = acc + _dot(w.astype(BF16), v)
            after = after + jnp.sum(lom, axis=-1, keepdims=True)
            return after, acc

        carry = step(i, (jnp.zeros((tq, 1), F32), jnp.zeros((tq, LANES), F32)), True)
        _, acc = lax.fori_loop(0, i, lambda n, c: step(i - 1 - n, c, False), carry)
        outs.append(acc)
    o_ref[...] = jnp.where(lane < HEAD_DIM, outs[0], outs[1]).astype(BF16)


def _sb(q, k, v, B, S, tq):
    nq = S // tq
    npair = SB_HEADS // 2
    return pl.pallas_call(
        functools.partial(_sb_kernel, tq=tq),
        grid=(B, npair, nq),
        in_specs=[pl.BlockSpec((tq, LANES), lambda b, j, i: (b * nq + i, j)),
                  pl.BlockSpec((S, LANES), lambda b, j, i: (b, j)),
                  pl.BlockSpec((S, LANES), lambda b, j, i: (b, j))],
        out_specs=pl.BlockSpec((tq, LANES), lambda b, j, i: (b * nq + i, j)),
        out_shape=jax.ShapeDtypeStruct((B * S, SB_W), BF16),
        name="sb",
    )(q, k, v)


def _merge_kernel(x_ref, gn_ref, oa_ref, lse_ref, ob_ref, oc_ref, wg_ref, wb_ref, wo_ref, out_ref):
    x = x_ref[...]
    h = _rms(x, gn_ref[...]).astype(BF16)
    l0, l1, l2 = lse_ref[0], lse_ref[1], lse_ref[2]
    m = jnp.maximum(jnp.maximum(l0, l1), l2)
    e0, e1, e2 = jnp.exp(l0 - m), jnp.exp(l1 - m), jnp.exp(l2 - m)
    oa = ((e0 * oa_ref[0] + e1 * oa_ref[1] + e2 * oa_ref[2]) / (e0 + e1 + e2)).astype(BF16)
    merged = None
    r0 = 0
    for b, o in enumerate((oa, ob_ref[...], oc_ref[...])):
        gate = jax.nn.sigmoid(_dot(h, wg_ref[:, b * D_MODEL:(b + 1) * D_MODEL]))
        term = gate * _dot(o, wb_ref[r0:r0 + o.shape[1], :])
        merged = term if merged is None else merged + term
        r0 += o.shape[1]
    out_ref[...] = x + _dot(merged.astype(BF16), wo_ref[...])


def _merge(x, p, oa, lse, ob, oc, tm):
    T = x.shape[0]
    row = lambda w: pl.BlockSpec((tm, w), lambda i: (i, 0))
    grp = pl.BlockSpec((len(DSA_GROUPS), tm, DSA_OUT), lambda i: (0, i, 0))
    consts = [p['wg'], p['wb'], p['wo']]
    return pl.pallas_call(
        _merge_kernel,
        grid=(T // tm,),
        in_specs=[row(D_MODEL), _const_spec(p['gn'].shape), grp, grp, row(MLA_OUT), row(SB_W)]
                 + [_const_spec(c.shape) for c in consts],
        out_specs=row(D_MODEL),
        out_shape=jax.ShapeDtypeStruct((T, D_MODEL), F32),
        compiler_params=pltpu.CompilerParams(vmem_limit_bytes=VMEM_LIMIT),
        name="merge",
    )(x, p['gn'], oa, lse, ob, oc, *consts)


def _mlp_kernel(x_ref, g_ref, w1_ref, w2_ref, out_ref, *, chunk):
    x = x_ref[...]
    h = _rms(x, g_ref[...]).astype(BF16)
    acc = x
    for c in range(D_FF // chunk):
        a = jnp.maximum(_dot(h, w1_ref[:, c * chunk:(c + 1) * chunk]), 0.0)
        acc = acc + _dot((a * a).astype(BF16), w2_ref[c * chunk:(c + 1) * chunk, :])
    out_ref[...] = acc


def _mlp(x, g, w1, w2, tm, chunk=512):
    T = x.shape[0]
    row = pl.BlockSpec((tm, D_MODEL), lambda i: (i, 0))
    return pl.pallas_call(
        functools.partial(_mlp_kernel, chunk=chunk),
        grid=(T // tm,),
        in_specs=[row, _const_spec(g.shape), _const_spec(w1.shape), _const_spec(w2.shape)],
        out_specs=row,
        out_shape=jax.ShapeDtypeStruct((T, D_MODEL), F32),
        compiler_params=pltpu.CompilerParams(vmem_limit_bytes=VMEM_LIMIT),
        name="mlp",
    )(x, g, w1, w2)


def _rope_lane_tables(S, dim, first_lane, period):
    half = dim // 2
    pos = jnp.arange(S, dtype=F32)
    inv_freq = ROPE_THETA ** (-jnp.arange(0, dim, 2, dtype=F32) / dim)
    ang = pos[:, None] * inv_freq[None, :]
    cos, sin = jnp.cos(ang), jnp.sin(ang)
    rel = jnp.arange(LANES) % period - first_lane
    idx = jnp.clip(rel, 0, dim - 1) % half
    in1 = (rel >= 0) & (rel < half)
    in2 = (rel >= half) & (rel < dim)
    c = jnp.where((in1 | in2)[None, :], cos[:, idx], 1.0)
    s1 = jnp.where(in2[None, :], sin[:, idx], 0.0)
    s2 = jnp.where(in1[None, :], -sin[:, idx], 0.0)
    return c, s1, s2


def _block_diag_ones(block):
    r = jnp.arange(MXU_DIM) // block
    return (r[:, None] == r[None, :]).astype(BF16)


def _prep_params(attn_norm, w_in, a_q_norm, a_k_norm, b_q_a_norm, w_q_b, b_kv_a_norm, w_kv_b,
                 b_q_norm, b_k_norm, w_branch, w_out, mlp_norm, w_ff1, w_ff2):
    L = w_in.shape[0]
    o_ql = 3 * DSA_W
    o_kr = o_ql + MLA_Q_LORA + MLA_KV_LORA
    o_c = o_kr + MLA_ROPE
    o_g = o_c + 3 * SB_W
    zeros = lambda n: jnp.zeros((L, D_MODEL, n), F32)
    ws = jnp.concatenate([w_in[:, :, o_ql:o_kr], zeros(MLA_NOPE), w_in[:, :, o_kr:o_c],
                          zeros(MLA_PAD - MLA_QK)], axis=-1)
    qscale = jnp.concatenate([jnp.full((SB_W,), HEAD_DIM ** -0.5, F32), jnp.ones((2 * SB_W,), F32)])
    pad_head = lambda w: jnp.pad(w, ((0, 0), (0, 0), (0, 0), (0, MLA_PAD - w.shape[-1]))).reshape(
        L, w.shape[1], MLA_HEADS * MLA_PAD)
    pad_gain = lambda g: jnp.tile(jnp.pad(g, ((0, 0), (0, MLA_PAD - MLA_QK))), (1, MLA_HEADS))[:, None, :]
    n_heads_a = DSA_W // HEAD_DIM
    return {
        'gn': attn_norm[:, None, :],
        'wa': w_in[:, :, :o_ql].astype(BF16),
        'ws': ws.astype(BF16),
        'wc': (w_in[:, :, o_c:o_g] * qscale).astype(BF16),
        'wg': w_in[:, :, o_g:].astype(BF16),
        'wqb': pad_head(w_q_b).astype(BF16),
        'wk': pad_head(w_kv_b[..., :MLA_NOPE]).astype(BF16),
        'wv': w_kv_b[..., MLA_NOPE:].reshape(L, MLA_KV_LORA, MLA_OUT).astype(BF16),
        'gaq': jnp.tile(a_q_norm * HEAD_DIM ** -0.5, (1, n_heads_a))[:, None, :],
        'gak': jnp.tile(a_k_norm, (1, n_heads_a))[:, None, :],
        'gql': b_q_a_norm[:, None, :],
        'gkvl': b_kv_a_norm[:, None, :],
        'gbq': pad_gain(b_q_norm * MLA_QK ** -0.5),
        'gbk': pad_gain(b_k_norm),
        'wb': w_branch.astype(BF16),
        'wo': w_out.astype(BF16),
        'gm': mlp_norm[:, None, :],
        'w1': w_ff1.astype(BF16),
        'w2': w_ff2.astype(BF16),
    }


def _dilate(t, B, S, d):
    if d == 1:
        return t
    W = t.shape[-1]
    return t.reshape(B, S // d, d, W).transpose(0, 2, 1, 3).reshape(B * S, W)


def _undilate(t, B, S, d):
    if d == 1:
        return t
    W = t.shape[-1]
    return t.reshape(B, d, S // d, W).transpose(0, 2, 1, 3).reshape(B * S, W)


def kernel(x, attn_norm, w_in, a_q_norm, a_k_norm, b_q_a_norm, w_q_b, b_kv_a_norm, w_kv_b, b_q_norm,
           b_k_norm, w_branch, w_out, mlp_norm, w_ff1, w_ff2):
    B, S, _ = x.shape
    depth = w_in.shape[0]
    params = _prep_params(attn_norm, w_in, a_q_norm, a_k_norm, b_q_a_norm, w_q_b, b_kv_a_norm, w_kv_b,
                          b_q_norm, b_k_norm, w_branch, w_out, mlp_norm, w_ff1, w_ff2)
    tabs = {
        'rope': _rope_lane_tables(S, ROT_DIM, 0, HEAD_DIM) + _rope_lane_tables(S, MLA_ROPE, MLA_NOPE, MLA_PAD),
        'm64': _block_diag_ones(HEAD_DIM),
        'm128': _block_diag_ones(MLA_PAD),
    }
    tm = min(512, S)
    xt = x.reshape(B * S, D_MODEL)
    for l in range(depth):
        p = {k: v[l] for k, v in params.items()}
        aq, ak, av, bq, bk, bv, cq, ck, cv = _proj(xt, p, tabs, S, tm)

        oas, lses = [], []
        for g, (window, d) in enumerate(DSA_GROUPS):
            cols = slice(g * DSA_OUT, (g + 1) * DSA_OUT)
            qg, kg, vg = (_dilate(t[:, cols], B, S, d) for t in (aq, ak, av))
            o, lse = _dsa(qg, kg, vg, S // d // BLOCK)
            oas.append(_undilate(o, B, S, d))
            lses.append(_undilate(lse, B, S, d))
        ob = _mla(bq, bk, bv, B, S, BLOCK)
        oc = _sb(cq, ck, cv, B, S, BLOCK)

        xt = _merge(xt, p, jnp.stack(oas), jnp.stack(lses), ob, oc, min(256, S))
        xt = _mlp(xt, p['gm'], p['w1'], p['w2'], tm)
    return xt.reshape(B, S, D_MODEL)
```

```python
import functools

import jax
import jax.numpy as jnp
from jax import lax
from jax.experimental import pallas as pl
from jax.experimental.pallas import tpu as pltpu

D_MODEL = 1024
HEAD_DIM = 64
ROPE_THETA = 500000.0
ROT_DIM = HEAD_DIM // 4
NORM_EPS = 1e-6
BLOCK = 128

DSA_GROUPS = ((128, 1), (512, 4), (2048, 16))
DSA_HEADS_PER_GROUP = 4
DSA_W = DSA_HEADS_PER_GROUP * len(DSA_GROUPS) * HEAD_DIM
DSA_OUT = DSA_HEADS_PER_GROUP * HEAD_DIM

MLA_HEADS = 8
MLA_Q_LORA = 256
MLA_KV_LORA = 128
MLA_NOPE = 64
MLA_ROPE = 32
MLA_QK = MLA_NOPE + MLA_ROPE
MLA_V = 64
MLA_OUT = MLA_HEADS * MLA_V
MLA_PAD = 128

SB_HEADS = 8
SB_W = SB_HEADS * HEAD_DIM

N_BRANCH = 3
D_FF = 4 * D_MODEL

LANES = 128
MXU_DIM = 256
VMEM_LIMIT = 56 * 1024 * 1024

F32 = jnp.float32
BF16 = jnp.bfloat16
NEG = -1e30


def _dot(a, b):
    return jnp.dot(a, b, preferred_element_type=F32)


def _dot_nt(a, b):
    return lax.dot_general(a, b, (((1,), (1,)), ((), ())), preferred_element_type=F32)


def _split_dot(s, m):
    hi = s.astype(BF16)
    lo = (s - hi.astype(F32)).astype(BF16)
    return _dot(hi, m) + _dot(lo, m)


def _rms(x, g):
    ms = jnp.mean(x * x, axis=-1, keepdims=True)
    return x * lax.rsqrt(ms + NORM_EPS) * g


def _head_norm(y, m, inv_n, g):
    ss = _split_dot(y * y, m)
    return y * lax.rsqrt(ss * inv_n + NORM_EPS) * g


def _rope(y, c, s1, s2, shift):
    outs = []
    for j in range(y.shape[1] // LANES):
        yj = y[:, j * LANES:(j + 1) * LANES]
        outs.append(yj * c + pltpu.roll(yj, shift, 1) * s1 + pltpu.roll(yj, LANES - shift, 1) * s2)
    return jnp.concatenate(outs, axis=-1)


def _proj_kernel(x_ref, gn_ref, wa_ref, ws_ref, wc_ref, wqb_ref, wk_ref, wv_ref,
                 gaq_ref, gak_ref, gql_ref, gkvl_ref, gbq_ref, gbk_ref,
                 ca_ref, s1a_ref, s2a_ref, cm_ref, s1m_ref, s2m_ref, m64_ref, m128_ref,
                 aq_ref, ak_ref, av_ref, bq_ref, bk_ref, bv_ref, cq_ref, ck_ref, cv_ref):
    h = _rms(x_ref[...], gn_ref[...]).astype(BF16)
    ca, s1a, s2a = ca_ref[...], s1a_ref[...], s2a_ref[...]
    cm, s1m, s2m = cm_ref[...], s1m_ref[...], s2m_ref[...]
    m64, m128 = m64_ref[...], m128_ref[...]
    W = MXU_DIM

    for dst, g_ref, off in ((aq_ref, gaq_ref, 0), (ak_ref, gak_ref, DSA_W)):
        for c in range(DSA_W // W):
            y = _dot(h, wa_ref[:, off + c * W: off + (c + 1) * W])
            y = _head_norm(y, m64, 1.0 / HEAD_DIM, g_ref[:, c * W:(c + 1) * W])
            dst[:, c * W:(c + 1) * W] = _rope(y, ca, s1a, s2a, ROT_DIM // 2).astype(BF16)
    av_ref[...] = _dot(h, wa_ref[:, 2 * DSA_W:3 * DSA_W]).astype(BF16)

    ys = _dot(h, ws_ref[...])
    ql = _rms(ys[:, :MLA_Q_LORA], gql_ref[...]).astype(BF16)
    kvl = _rms(ys[:, MLA_Q_LORA:MLA_Q_LORA + MLA_KV_LORA], gkvl_ref[...]).astype(BF16)
    kr = ys[:, MLA_Q_LORA + MLA_KV_LORA:]
    kr2 = jnp.concatenate([kr, kr], axis=-1)
    for c in range(MLA_HEADS * MLA_PAD // W):
        yq = _dot(ql, wqb_ref[:, c * W:(c + 1) * W])
        yq = _head_norm(yq, m128, 1.0 / MLA_QK, gbq_ref[:, c * W:(c + 1) * W])
        bq_ref[:, c * W:(c + 1) * W] = _rope(yq, cm, s1m, s2m, MLA_ROPE // 2).astype(BF16)
        yk = _dot(kvl, wk_ref[:, c * W:(c + 1) * W]) + kr2
        yk = _head_norm(yk, m128, 1.0 / MLA_QK, gbk_ref[:, c * W:(c + 1) * W])
        bk_ref[:, c * W:(c + 1) * W] = _rope(yk, cm, s1m, s2m, MLA_ROPE // 2).astype(BF16)
    bv_ref[...] = _dot(kvl, wv_ref[...]).astype(BF16)

    cq_ref[...] = _dot(h, wc_ref[:, :SB_W]).astype(BF16)
    ck_ref[...] = _dot(h, wc_ref[:, SB_W:2 * SB_W]).astype(BF16)
    cv_ref[...] = _dot(h, wc_ref[:, 2 * SB_W:]).astype(BF16)


def _const_spec(shape):
    nd = len(shape)
    return pl.BlockSpec(shape, lambda *_: (0,) * nd)


def _proj(x, p, tabs, S, tm):
    T = x.shape[0]
    ns = S // tm
    row = lambda w: pl.BlockSpec((tm, w), lambda i: (i, 0))
    tab = pl.BlockSpec((tm, LANES), lambda i: (i % ns, 0))
    consts = [p['gn'], p['wa'], p['ws'], p['wc'], p['wqb'], p['wk'], p['wv'],
              p['gaq'], p['gak'], p['gql'], p['gkvl'], p['gbq'], p['gbk']]
    ins = [x] + consts + list(tabs['rope']) + [tabs['m64'], tabs['m128']]
    in_specs = ([row(D_MODEL)] + [_const_spec(c.shape) for c in consts] + [tab] * 6
                + [_const_spec((MXU_DIM, MXU_DIM))] * 2)
    widths = (DSA_W, DSA_W, DSA_W, MLA_HEADS * MLA_PAD, MLA_HEADS * MLA_PAD, MLA_OUT, SB_W, SB_W, SB_W)
    return pl.pallas_call(
        _proj_kernel,
        grid=(T // tm,),
        in_specs=in_specs,
        out_specs=[row(w) for w in widths],
        out_shape=[jax.ShapeDtypeStruct((T, w), BF16) for w in widths],
        compiler_params=pltpu.CompilerParams(vmem_limit_bytes=VMEM_LIMIT),
        name="proj",
    )(*ins)


def _dsa_kernel(q_ref, kp_ref, kc_ref, vp_ref, vc_ref, o_ref, lse_ref, *, nb_seq):
    n = pl.program_id(0)
    pen = jnp.where(n % nb_seq == 0, NEG, 0.0).astype(F32)
    qi = lax.broadcasted_iota(jnp.int32, (BLOCK, BLOCK), 0)
    kj = lax.broadcasted_iota(jnp.int32, (BLOCK, BLOCK), 1)
    for pr in range(DSA_OUT // LANES):
        sl = slice(pr * LANES, (pr + 1) * LANES)
        q, kp, kc, vp, vc = q_ref[:, sl], kp_ref[:, sl], kc_ref[:, sl], vp_ref[:, sl], vc_ref[:, sl]
        o_acc = jnp.zeros((BLOCK, LANES), F32)
        l_acc = jnp.zeros((BLOCK, LANES), F32)
        for hh in range(2):
            sel = (kj < HEAD_DIM) if hh == 0 else (kj >= HEAD_DIM)
            qh = jnp.where(sel, q, jnp.zeros_like(q))
            sp = jnp.where(kj >= qi, _dot_nt(qh, kp) + pen, NEG)
            sc = jnp.where(kj <= qi, _dot_nt(qh, kc), NEG)
            m = jnp.maximum(jnp.max(sp, axis=-1, keepdims=True), jnp.max(sc, axis=-1, keepdims=True))
            pp = jnp.exp(sp - m)
            pc = jnp.exp(sc - m)
            den = jnp.sum(pp, axis=-1, keepdims=True) + jnp.sum(pc, axis=-1, keepdims=True)
            o = (_dot(pp.astype(BF16), vp) + _dot(pc.astype(BF16), vc)) / den
            lse = m + jnp.log(den)
            o_acc = jnp.where(sel, o, o_acc)
            l_acc = jnp.where(sel, lse, l_acc)
        o_ref[:, sl] = o_acc
        lse_ref[:, sl] = l_acc


def _dsa(q, k, v, nb_seq):
    T = q.shape[0]
    cur = pl.BlockSpec((BLOCK, DSA_OUT), lambda n: (n, 0))
    prev = pl.BlockSpec((BLOCK, DSA_OUT), lambda n: (jnp.maximum(n - 1, 0), 0))
    return pl.pallas_call(
        functools.partial(_dsa_kernel, nb_seq=nb_seq),
        grid=(T // BLOCK,),
        in_specs=[cur, prev, cur, prev, cur],
        out_specs=[cur, cur],
        out_shape=[jax.ShapeDtypeStruct((T, DSA_OUT), F32)] * 2,
        name="dsa",
    )(q, k, k, v, v)


def _mla_kernel(q_ref, k_ref, v_ref, o_ref, *, tq):
    i = pl.program_id(2)
    row = lax.broadcasted_iota(jnp.int32, (tq, tq), 0)
    col = lax.broadcasted_iota(jnp.int32, (tq, tq), 1)
    lane = lax.broadcasted_iota(jnp.int32, (tq, LANES), 1)
    qs = [q_ref[:, hh * MLA_PAD:(hh + 1) * MLA_PAD] for hh in range(2)]

    def step(kb, carry, masked):
        start = pl.multiple_of(kb * tq, tq)
        k = k_ref[pl.ds(start, tq), :]
        v = v_ref[pl.ds(start, tq), :]
        new = []
        for hh in range(2):
            m, l, acc = carry[hh]
            s = _dot_nt(qs[hh], k[:, hh * MLA_PAD:(hh + 1) * MLA_PAD])
            if masked:
                s = jnp.where(col <= row, s, NEG)
            m_new = jnp.maximum(m, jnp.max(s, axis=-1, keepdims=True))
            alpha = jnp.exp(m - m_new)
            p = jnp.exp(s - m_new)
            l = l * alpha + jnp.sum(p, axis=-1, keepdims=True)
            acc = acc * alpha + _dot(p.astype(BF16), v)
            new.append((m_new, l, acc))
        return tuple(new)

    init = (jnp.full((tq, 1), NEG, F32), jnp.zeros((tq, 1), F32), jnp.zeros((tq, LANES), F32))
    carry = lax.fori_loop(0, i, lambda kb, c: step(kb, c, False), (init, init))
    (_, l0, acc0), (_, l1, acc1) = step(i, carry, True)
    o_ref[...] = jnp.where(lane < MLA_V, acc0 / l0, acc1 / l1).astype(BF16)


def _mla(q, k, v, B, S, tq):
    nq = S // tq
    npair = MLA_HEADS // 2
    return pl.pallas_call(
        functools.partial(_mla_kernel, tq=tq),
        grid=(B, npair, nq),
        in_specs=[pl.BlockSpec((tq, 2 * MLA_PAD), lambda b, j, i: (b * nq + i, j)),
                  pl.BlockSpec((S, 2 * MLA_PAD), lambda b, j, i: (b, j)),
                  pl.BlockSpec((S, LANES), lambda b, j, i: (b, j))],
        out_specs=pl.BlockSpec((tq, LANES), lambda b, j, i: (b * nq + i, j)),
        out_shape=jax.ShapeDtypeStruct((B * S, MLA_OUT), BF16),
        compiler_params=pltpu.CompilerParams(vmem_limit_bytes=VMEM_LIMIT),
        name="mla",
    )(q, k, v)


def _sb_kernel(q_ref, k_ref, v_ref, o_ref, *, tq):
    i = pl.program_id(2)
    row = lax.broadcasted_iota(jnp.int32, (tq, tq), 0)
    col = lax.broadcasted_iota(jnp.int32, (tq, tq), 1)
    lane = lax.broadcasted_iota(jnp.int32, (tq, LANES), 1)
    strict = col < row
    tri = jnp.where(row > col, 1.0, 0.0).astype(BF16)
    q = q_ref[...]
    qs = [jnp.where(lane < HEAD_DIM, q, jnp.zeros_like(q)), jnp.where(lane >= HEAD_DIM, q, jnp.zeros_like(q))]

    def step(kb, carry, masked):
        start = pl.multiple_of(kb * tq, tq)
        k = k_ref[pl.ds(start, tq), :]
        v = v_ref[pl.ds(start, tq), :]
        new = []
        for hh in range(2):
            after, acc = carry[hh]
            z = _dot_nt(qs[hh], k)
            log_beta = jnp.minimum(z, 0.0) - jnp.log(1.0 + jnp.exp(-jnp.abs(z)))
            lom = log_beta - z
            if masked:
                lom = jnp.where(strict, lom, 0.0)
            w = jnp.exp(log_beta + _split_dot(lom, tri) + after)
            if masked:
                w = jnp.where(strict, w, 0.0)
            acc = acc + _dot(w.astype(BF16), v)
            after = after + jnp.sum(lom, axis=-1, keepdims=True)
            new.append((after, acc))
        return tuple(new)

    init = (jnp.zeros((tq, 1), F32), jnp.zeros((tq, LANES), F32))
    carry = step(i, (init, init), True)
    (_, acc0), (_, acc1) = lax.fori_loop(0, i, lambda n, c: step(i - 1 - n, c, False), carry)
    o_ref[...] = jnp.where(lane < HEAD_DIM, acc0, acc1).astype(BF16)


def _sb(q, k, v, B, S, tq):
    nq = S // tq
    npair = SB_HEADS // 2
    return pl.pallas_call(
        functools.partial(_sb_kernel, tq=tq),
        grid=(B, npair, nq),
        in_specs=[pl.BlockSpec((tq, LANES), lambda b, j, i: (b * nq + i, j)),
                  pl.BlockSpec((S, LANES), lambda b, j, i: (b, j)),
                  pl.BlockSpec((S, LANES), lambda b, j, i: (b, j))],
        out_specs=pl.BlockSpec((tq, LANES), lambda b, j, i: (b * nq + i, j)),
        out_shape=jax.ShapeDtypeStruct((B * S, SB_W), BF16),
        compiler_params=pltpu.CompilerParams(vmem_limit_bytes=VMEM_LIMIT),
        name="sb",
    )(q, k, v)


def _merge_kernel(x_ref, gn_ref, oa_ref, lse_ref, ob_ref, oc_ref, wg_ref, wb_ref, wo_ref, out_ref):
    x = x_ref[...]
    h = _rms(x, gn_ref[...]).astype(BF16)
    l0, l1, l2 = lse_ref[0], lse_ref[1], lse_ref[2]
    m = jnp.maximum(jnp.maximum(l0, l1), l2)
    e0, e1, e2 = jnp.exp(l0 - m), jnp.exp(l1 - m), jnp.exp(l2 - m)
    oa = ((e0 * oa_ref[0] + e1 * oa_ref[1] + e2 * oa_ref[2]) / (e0 + e1 + e2)).astype(BF16)
    merged = None
    r0 = 0
    for b, o in enumerate((oa, ob_ref[...], oc_ref[...])):
        gate = jax.nn.sigmoid(_dot(h, wg_ref[:, b * D_MODEL:(b + 1) * D_MODEL]))
        term = gate * _dot(o, wb_ref[r0:r0 + o.shape[1], :])
        merged = term if merged is None else merged + term
        r0 += o.shape[1]
    out_ref[...] = x + _dot(merged.astype(BF16), wo_ref[...])


def _merge(x, p, oa, lse, ob, oc, tm):
    T = x.shape[0]
    row = lambda w: pl.BlockSpec((tm, w), lambda i: (i, 0))
    grp = pl.BlockSpec((len(DSA_GROUPS), tm, DSA_OUT), lambda i: (0, i, 0))
    consts = [p['wg'], p['wb'], p['wo']]
    return pl.pallas_call(
        _merge_kernel,
        grid=(T // tm,),
        in_specs=[row(D_MODEL), _const_spec(p['gn'].shape), grp, grp, row(MLA_OUT), row(SB_W)]
                 + [_const_spec(c.shape) for c in consts],
        out_specs=row(D_MODEL),
        out_shape=jax.ShapeDtypeStruct((T, D_MODEL), F32),
        compiler_params=pltpu.CompilerParams(vmem_limit_bytes=VMEM_LIMIT),
        name="merge",
    )(x, p['gn'], oa, lse, ob, oc, *consts)


def _mlp_kernel(x_ref, g_ref, w1_ref, w2_ref, out_ref, *, chunk):
    x = x_ref[...]
    h = _rms(x, g_ref[...]).astype(BF16)
    acc = x
    for c in range(D_FF // chunk):
        a = jnp.maximum(_dot(h, w1_ref[:, c * chunk:(c + 1) * chunk]), 0.0)
        acc = acc + _dot((a * a).astype(BF16), w2_ref[c * chunk:(c + 1) * chunk, :])
    out_ref[...] = acc


def _mlp(x, g, w1, w2, tm, chunk=512):
    T = x.shape[0]
    row = pl.BlockSpec((tm, D_MODEL), lambda i: (i, 0))
    return pl.pallas_call(
        functools.partial(_mlp_kernel, chunk=chunk),
        grid=(T // tm,),
        in_specs=[row, _const_spec(g.shape), _const_spec(w1.shape), _const_spec(w2.shape)],
        out_specs=row,
        out_shape=jax.ShapeDtypeStruct((T, D_MODEL), F32),
        compiler_params=pltpu.CompilerParams(vmem_limit_bytes=VMEM_LIMIT),
        name="mlp",
    )(x, g, w1, w2)


def _rope_lane_tables(S, dim, first_lane, period):
    half = dim // 2
    pos = jnp.arange(S, dtype=F32)
    inv_freq = ROPE_THETA ** (-jnp.arange(0, dim, 2, dtype=F32) / dim)
    ang = pos[:, None] * inv_freq[None, :]
    cos, sin = jnp.cos(ang), jnp.sin(ang)
    rel = jnp.arange(LANES) % period - first_lane
    idx = jnp.clip(rel, 0, dim - 1) % half
    in1 = (rel >= 0) & (rel < half)
    in2 = (rel >= half) & (rel < dim)
    c = jnp.where((in1 | in2)[None, :], cos[:, idx], 1.0)
    s1 = jnp.where(in2[None, :], sin[:, idx], 0.0)
    s2 = jnp.where(in1[None, :], -sin[:, idx], 0.0)
    return c, s1, s2


def _block_diag_ones(block):
    r = jnp.arange(MXU_DIM) // block
    return (r[:, None] == r[None, :]).astype(BF16)


def _prep_params(attn_norm, w_in, a_q_norm, a_k_norm, b_q_a_norm, w_q_b, b_kv_a_norm, w_kv_b,
                 b_q_norm, b_k_norm, w_branch, w_out, mlp_norm, w_ff1, w_ff2):
    L = w_in.shape[0]
    o_ql = 3 * DSA_W
    o_kr = o_ql + MLA_Q_LORA + MLA_KV_LORA
    o_c = o_kr + MLA_ROPE
    o_g = o_c + 3 * SB_W
    zeros = lambda n: jnp.zeros((L, D_MODEL, n), F32)
    ws = jnp.concatenate([w_in[:, :, o_ql:o_kr], zeros(MLA_NOPE), w_in[:, :, o_kr:o_c],
                          zeros(MLA_PAD - MLA_QK)], axis=-1)
    qscale = jnp.concatenate([jnp.full((SB_W,), HEAD_DIM ** -0.5, F32), jnp.ones((2 * SB_W,), F32)])
    pad_head = lambda w: jnp.pad(w, ((0, 0), (0, 0), (0, 0), (0, MLA_PAD - w.shape[-1]))).reshape(
        L, w.shape[1], MLA_HEADS * MLA_PAD)
    pad_gain = lambda g: jnp.tile(jnp.pad(g, ((0, 0), (0, MLA_PAD - MLA_QK))), (1, MLA_HEADS))[:, None, :]
    n_heads_a = DSA_W // HEAD_DIM
    return {
        'gn': attn_norm[:, None, :],
        'wa': w_in[:, :, :o_ql].astype(BF16),
        'ws': ws.astype(BF16),
        'wc': (w_in[:, :, o_c:o_g] * qscale).astype(BF16),
        'wg': w_in[:, :, o_g:].astype(BF16),
        'wqb': pad_head(w_q_b).astype(BF16),
        'wk': pad_head(w_kv_b[..., :MLA_NOPE]).astype(BF16),
        'wv': w_kv_b[..., MLA_NOPE:].reshape(L, MLA_KV_LORA, MLA_OUT).astype(BF16),
        'gaq': jnp.tile(a_q_norm * HEAD_DIM ** -0.5, (1, n_heads_a))[:, None, :],
        'gak': jnp.tile(a_k_norm, (1, n_heads_a))[:, None, :],
        'gql': b_q_a_norm[:, None, :],
        'gkvl': b_kv_a_norm[:, None, :],
        'gbq': pad_gain(b_q_norm * MLA_QK ** -0.5),
        'gbk': pad_gain(b_k_norm),
        'wb': w_branch.astype(BF16),
        'wo': w_out.astype(BF16),
        'gm': mlp_norm[:, None, :],
        'w1': w_ff1.astype(BF16),
        'w2': w_ff2.astype(BF16),
    }


def _dilate(t, B, S, d):
    if d == 1:
        return t
    W = t.shape[-1]
    return t.reshape(B, S // d, d, W).transpose(0, 2, 1, 3).reshape(B * S, W)


def _undilate(t, B, S, d):
    if d == 1:
        return t
    W = t.shape[-1]
    return t.reshape(B, d, S // d, W).transpose(0, 2, 1, 3).reshape(B * S, W)


def kernel(x, attn_norm, w_in, a_q_norm, a_k_norm, b_q_a_norm, w_q_b, b_kv_a_norm, w_kv_b, b_q_norm,
           b_k_norm, w_branch, w_out, mlp_norm, w_ff1, w_ff2):
    B, S, _ = x.shape
    depth = w_in.shape[0]
    params = _prep_params(attn_norm, w_in, a_q_norm, a_k_norm, b_q_a_norm, w_q_b, b_kv_a_norm, w_kv_b,
                          b_q_norm, b_k_norm, w_branch, w_out, mlp_norm, w_ff1, w_ff2)
    tabs = {
        'rope': _rope_lane_tables(S, ROT_DIM, 0, HEAD_DIM) + _rope_lane_tables(S, MLA_ROPE, MLA_NOPE, MLA_PAD),
        'm64': _block_diag_ones(HEAD_DIM),
        'm128': _block_diag_ones(MLA_PAD),
    }
    tm = min(512, S)
    xt = x.reshape(B * S, D_MODEL)
    for l in range(depth):
        p = {k: v[l] for k, v in params.items()}
        aq, ak, av, bq, bk, bv, cq, ck, cv = _proj(xt, p, tabs, S, tm)

        oas, lses = [], []
        for g, (window, d) in enumerate(DSA_GROUPS):
            cols = slice(g * DSA_OUT, (g + 1) * DSA_OUT)
            qg, kg, vg = (_dilate(t[:, cols], B, S, d) for t in (aq, ak, av))
            o, lse = _dsa(qg, kg, vg, S // d // BLOCK)
            oas.append(_undilate(o, B, S, d))
            lses.append(_undilate(lse, B, S, d))
        ob = _mla(bq, bk, bv, B, S, min(512, S))
        oc = _sb(cq, ck, cv, B, S, min(256, S))

        xt = _merge(xt, p, jnp.stack(oas), jnp.stack(lses), ob, oc, min(256, S))
        xt = _mlp(xt, p['gm'], p['w1'], p['w2'], tm)
    return xt.reshape(B, S, D_MODEL)
```

```python
import functools

import jax
import jax.numpy as jnp
from jax import lax
from jax.experimental import pallas as pl
from jax.experimental.pallas import tpu as pltpu

D_MODEL = 1024
HEAD_DIM = 64
ROPE_THETA = 500000.0
ROT_DIM = HEAD_DIM // 4
NORM_EPS = 1e-6
BLOCK = 128

DSA_GROUPS = ((128, 1), (512, 4), (2048, 16))
DSA_HEADS_PER_GROUP = 4
DSA_W = DSA_HEADS_PER_GROUP * len(DSA_GROUPS) * HEAD_DIM
DSA_OUT = DSA_HEADS_PER_GROUP * HEAD_DIM

MLA_HEADS = 8
MLA_Q_LORA = 256
MLA_KV_LORA = 128
MLA_NOPE = 64
MLA_ROPE = 32
MLA_QK = MLA_NOPE + MLA_ROPE
MLA_V = 64
MLA_OUT = MLA_HEADS * MLA_V
MLA_PAD = 128

SB_HEADS = 8
SB_W = SB_HEADS * HEAD_DIM

N_BRANCH = 3
D_FF = 4 * D_MODEL

LANES = 128
MXU_DIM = 256
VMEM_LIMIT = 56 * 1024 * 1024
MLA_TQ, MLA_TK = 512, 512

F32 = jnp.float32
BF16 = jnp.bfloat16
NEG = -1e30
LOG2E = 1.4426950408889634
SB_DEAD_LOG2 = -160.0


def _dot(a, b):
    return jnp.dot(a, b, preferred_element_type=F32)


def _dot_nt(a, b):
    return lax.dot_general(a, b, (((1,), (1,)), ((), ())), preferred_element_type=F32)


def _split_dot(s, m):
    hi = s.astype(BF16)
    lo = (s - hi.astype(F32)).astype(BF16)
    return _dot(hi, m) + _dot(lo, m)


def _rms(x, g):
    ms = jnp.mean(x * x, axis=-1, keepdims=True)
    return x * lax.rsqrt(ms + NORM_EPS) * g


def _head_norm(y, m, inv_n, g):
    ss = _split_dot(y * y, m)
    return y * lax.rsqrt(ss * inv_n + NORM_EPS) * g


def _rope(y, c, s1, s2, shift):
    outs = []
    for j in range(y.shape[1] // LANES):
        yj = y[:, j * LANES:(j + 1) * LANES]
        outs.append(yj * c + pltpu.roll(yj, shift, 1) * s1 + pltpu.roll(yj, LANES - shift, 1) * s2)
    return jnp.concatenate(outs, axis=-1)


def _proj_kernel(x_ref, gn_ref, wa_ref, ws_ref, wc_ref, wqb_ref, wk_ref, wv_ref,
                 gaq_ref, gak_ref, gql_ref, gkvl_ref, gbq_ref, gbk_ref,
                 ca_ref, s1a_ref, s2a_ref, cm_ref, s1m_ref, s2m_ref, m64_ref, m128_ref,
                 *rest):
    n_a = 3 * len(DSA_GROUPS)
    a_refs, (bq_ref, bk_ref, bv_ref, cq_ref, ck_ref, cv_ref) = rest[:n_a], rest[n_a:n_a + 6]
    scr_refs = rest[n_a + 6:]
    h = _rms(x_ref[...], gn_ref[...]).astype(BF16)
    ca, s1a, s2a = ca_ref[...], s1a_ref[...], s2a_ref[...]
    cm, s1m, s2m = cm_ref[...], s1m_ref[...], s2m_ref[...]
    m64, m128 = m64_ref[...], m128_ref[...]
    W = MXU_DIM
    tm = x_ref.shape[0]

    scr = iter(scr_refs)
    for t, g_ref in enumerate((gaq_ref, gak_ref, None)):
        for g, (_, d) in enumerate(DSA_GROUPS):
            y = _dot(h, wa_ref[:, t * DSA_W + g * W: t * DSA_W + (g + 1) * W])
            if g_ref is not None:
                y = _head_norm(y, m64, 1.0 / HEAD_DIM, g_ref[:, g * W:(g + 1) * W])
                y = _rope(y, ca, s1a, s2a, ROT_DIM // 2)
            dst = a_refs[t * len(DSA_GROUPS) + g]
            if d == 1:
                dst[0] = y.astype(BF16)
            else:
                buf = next(scr)
                for half in range(W // LANES):
                    buf[half] = y[:, half * LANES:(half + 1) * LANES]
                for r in range(d):
                    for half in range(W // LANES):
                        dst[r, :, half * LANES:(half + 1) * LANES] = (
                            buf[half, pl.ds(r, tm // d, stride=d), :].astype(BF16))

    ys = _dot(h, ws_ref[...])
    ql = _rms(ys[:, :MLA_Q_LORA], gql_ref[...]).astype(BF16)
    kvl = _rms(ys[:, MLA_Q_LORA:MLA_Q_LORA + MLA_KV_LORA], gkvl_ref[...]).astype(BF16)
    kr = ys[:, MLA_Q_LORA + MLA_KV_LORA:]
    kr2 = jnp.concatenate([kr, kr], axis=-1)
    for c in range(MLA_HEADS * MLA_PAD // W):
        yq = _dot(ql, wqb_ref[:, c * W:(c + 1) * W])
        yq = _head_norm(yq, m128, 1.0 / MLA_QK, gbq_ref[:, c * W:(c + 1) * W])
        bq_ref[:, c * W:(c + 1) * W] = _rope(yq, cm, s1m, s2m, MLA_ROPE // 2).astype(BF16)
        yk = _dot(kvl, wk_ref[:, c * W:(c + 1) * W]) + kr2
        yk = _head_norm(yk, m128, 1.0 / MLA_QK, gbk_ref[:, c * W:(c + 1) * W])
        bk_ref[:, c * W:(c + 1) * W] = _rope(yk, cm, s1m, s2m, MLA_ROPE // 2).astype(BF16)
    bv_ref[...] = _dot(kvl, wv_ref[...]).astype(BF16)

    cq_ref[...] = (_dot(h, wc_ref[:, :SB_W]) * (HEAD_DIM ** -0.5 * LOG2E)).astype(BF16)
    ck_ref[...] = _dot(h, wc_ref[:, SB_W:2 * SB_W]).astype(BF16)
    cv_ref[...] = _dot(h, wc_ref[:, 2 * SB_W:]).astype(BF16)


def _const_spec(shape):
    nd = len(shape)
    return pl.BlockSpec(shape, lambda *_: (0,) * nd)


def _proj(x, p, tabs, S, tm):
    T = x.shape[0]
    ns = S // tm
    row = lambda w: pl.BlockSpec((tm, w), lambda i: (i, 0))
    tab = pl.BlockSpec((tm, LANES), lambda i: (i % ns, 0))
    consts = [p['gn'], p['wa'], p['ws'], p['wc'], p['wqb'], p['wk'], p['wv'],
              p['gaq'], p['gak'], p['gql'], p['gkvl'], p['gbq'], p['gbk']]
    ins = [x] + consts + list(tabs['rope']) + [tabs['m64'], tabs['m128']]
    in_specs = ([row(D_MODEL)] + [_const_spec(c.shape) for c in consts] + [tab] * 6
                + [_const_spec((MXU_DIM, MXU_DIM))] * 2)
    widths = (MLA_HEADS * MLA_PAD, MLA_HEADS * MLA_PAD, MLA_OUT, SB_W, SB_W, SB_W)
    B = T // S
    dils = [d for _ in range(3) for _, d in DSA_GROUPS]
    a_specs = [pl.BlockSpec((None, d, tm // d, DSA_OUT), lambda i: (i // ns, 0, i % ns, 0)) for d in dils]
    a_shapes = [jax.ShapeDtypeStruct((B, d, S // d, DSA_OUT), BF16) for d in dils]
    outs = pl.pallas_call(
        _proj_kernel,
        grid=(T // tm,),
        in_specs=in_specs,
        out_specs=a_specs + [row(w) for w in widths],
        out_shape=a_shapes + [jax.ShapeDtypeStruct((T, w), BF16) for w in widths],
        scratch_shapes=[pltpu.VMEM((DSA_OUT // LANES, tm, LANES), F32) for d in dils if d > 1],
        compiler_params=pltpu.CompilerParams(vmem_limit_bytes=VMEM_LIMIT),
        name="proj",
    )(*ins)
    return outs[:len(dils)], outs[len(dils):]


def _dsa_kernel(q_ref, kp_ref, kc_ref, vp_ref, vc_ref, o_ref, lse_ref, *, nb_seq):
    n = pl.program_id(0)
    pens = [jnp.where((2 * n + blk) % nb_seq == 0, NEG, 0.0).astype(F32) for blk in range(2)]
    qi = lax.broadcasted_iota(jnp.int32, (BLOCK, 2 * BLOCK), 0)
    kj = lax.broadcasted_iota(jnp.int32, (BLOCK, 2 * BLOCK), 1)
    band = (kj >= qi) & (kj <= qi + BLOCK)
    lane = lax.broadcasted_iota(jnp.int32, (BLOCK, LANES), 1)
    for blk in range(2):
        rows = slice(blk * BLOCK, (blk + 1) * BLOCK)
        pen_row = jnp.where(kj < BLOCK, pens[blk], 0.0)
        for pr in range(DSA_OUT // LANES):
            sl = slice(pr * LANES, (pr + 1) * LANES)
            q = q_ref[rows, sl]
            if blk == 0:
                k2 = jnp.concatenate([kp_ref[:, sl], kc_ref[:BLOCK, sl]], axis=0)
                v2 = jnp.concatenate([vp_ref[:, sl], vc_ref[:BLOCK, sl]], axis=0)
            else:
                k2, v2 = kc_ref[:, sl], vc_ref[:, sl]
            o_acc = l_acc = None
            for hh in range(2):
                sel = (lane < HEAD_DIM) if hh == 0 else (lane >= HEAD_DIM)
                qh = jnp.where(sel, q, jnp.zeros_like(q))
                s = jnp.where(band, _dot_nt(qh, k2) + pen_row, NEG)
                m = jnp.max(s, axis=-1, keepdims=True)
                p = jnp.exp(s - m)
                den = jnp.sum(p, axis=-1, keepdims=True)
                o = _dot(p.astype(BF16), v2) / den
                lse = m + jnp.log(den)
                o_acc = o if hh == 0 else jnp.where(sel, o, o_acc)
                l_acc = jnp.broadcast_to(lse, o.shape) if hh == 0 else jnp.where(sel, lse, l_acc)
            o_ref[rows, sl] = o_acc
            lse_ref[rows, sl] = l_acc


def _dsa(q, k, v, nb_seq):
    T = q.shape[0]
    cur = pl.BlockSpec((2 * BLOCK, DSA_OUT), lambda n: (n, 0))
    prev = pl.BlockSpec((BLOCK, DSA_OUT), lambda n: (jnp.maximum(2 * n - 1, 0), 0))
    return pl.pallas_call(
        functools.partial(_dsa_kernel, nb_seq=nb_seq),
        grid=(T // (2 * BLOCK),),
        in_specs=[cur, prev, cur, prev, cur],
        out_specs=[cur, cur],
        out_shape=[jax.ShapeDtypeStruct((T, DSA_OUT), F32)] * 2,
        name="dsa",
    )(q, k, k, v, v)


def _mla_kernel(q_ref, k_ref, v_ref, o_ref, *, tq, tk):
    i = pl.program_id(2)
    row = lax.broadcasted_iota(jnp.int32, (tq, tk), 0)
    col = lax.broadcasted_iota(jnp.int32, (tq, tk), 1)
    lane = lax.broadcasted_iota(jnp.int32, (tq, LANES), 1)
    qs = [q_ref[:, hh * MLA_PAD:(hh + 1) * MLA_PAD] for hh in range(2)]
    ndiag = tq // tk

    def step(kb, carry, diag):
        start = pl.multiple_of(kb * tk, tk)
        k = k_ref[pl.ds(start, tk), :]
        v = v_ref[pl.ds(start, tk), :]
        new = []
        for hh in range(2):
            m, l, acc = carry[hh]
            s = _dot_nt(qs[hh], k[:, hh * MLA_PAD:(hh + 1) * MLA_PAD])
            if diag is not None:
                s = jnp.where(col + diag * tk <= row, s, NEG)
            m_new = jnp.maximum(m, jnp.max(s, axis=-1, keepdims=True))
            alpha = jnp.exp2(m - m_new)
            p = jnp.exp2(s - m_new)
            l = l * alpha + jnp.sum(p, axis=-1, keepdims=True)
            acc = acc * alpha + _dot(p.astype(BF16), v)
            new.append((m_new, l, acc))
        return tuple(new)

    init = (jnp.full((tq, 1), NEG, F32), jnp.zeros((tq, 1), F32), jnp.zeros((tq, LANES), F32))
    carry = lax.fori_loop(0, i * ndiag, lambda kb, c: step(kb, c, None), (init, init))
    for j in range(ndiag):
        carry = step(i * ndiag + j, carry, j)
    (_, l0, acc0), (_, l1, acc1) = carry
    o_ref[...] = jnp.where(lane < MLA_V, acc0 / l0, acc1 / l1).astype(BF16)


def _mla(q, k, v, B, S, tq, tk):
    nq = S // tq
    npair = MLA_HEADS // 2
    return pl.pallas_call(
        functools.partial(_mla_kernel, tq=tq, tk=tk),
        grid=(B, npair, nq),
        in_specs=[pl.BlockSpec((tq, 2 * MLA_PAD), lambda b, j, i: (b * nq + i, j)),
                  pl.BlockSpec((S, 2 * MLA_PAD), lambda b, j, i: (b, j)),
                  pl.BlockSpec((S, LANES), lambda b, j, i: (b, j))],
        out_specs=pl.BlockSpec((tq, LANES), lambda b, j, i: (b * nq + i, j)),
        out_shape=jax.ShapeDtypeStruct((B * S, MLA_OUT), BF16),
        compiler_params=pltpu.CompilerParams(vmem_limit_bytes=VMEM_LIMIT),
        name="mla",
    )(q, k, v)


def _sb_kernel(q_ref, k_ref, v_ref, o_ref, *, tq):
    i = pl.program_id(2)
    row = lax.broadcasted_iota(jnp.int32, (tq, tq), 0)
    col = lax.broadcasted_iota(jnp.int32, (tq, tq), 1)
    lane = lax.broadcasted_iota(jnp.int32, (tq, LANES), 1)
    strict = col < row
    tri = jnp.where(row > col, 1.0, 0.0).astype(BF16)
    q = q_ref[...]
    qs = [jnp.where(lane < HEAD_DIM, q, jnp.zeros_like(q)), jnp.where(lane >= HEAD_DIM, q, jnp.zeros_like(q))]

    def step(kb, carry, masked):
        start = pl.multiple_of(kb * tq, tq)
        k = k_ref[pl.ds(start, tq), :]
        v = v_ref[pl.ds(start, tq), :]
        new = []
        for hh in range(2):
            after, acc = carry[hh]
            z = _dot_nt(qs[hh], k)
            log_beta = jnp.minimum(z, 0.0) - jnp.log(1.0 + jnp.exp2(-jnp.abs(z))) * LOG2E
            lom = log_beta - z
            if masked:
                lom = jnp.where(strict, lom, 0.0)
            w = jnp.exp2(log_beta + _split_dot(lom, tri) + after)
            if masked:
                w = jnp.where(strict, w, 0.0)
            acc = acc + _dot(w.astype(BF16), v)
            after = after + jnp.sum(lom, axis=-1, keepdims=True)
            new.append((after, acc))
        return tuple(new)

    def least_decayed(carry):
        return jnp.max(jnp.maximum(carry[0][0], carry[1][0]))

    def cond(state):
        n, top, _ = state
        return jnp.logical_and(n < i, top > SB_DEAD_LOG2)

    def body(state):
        n, _, carry = state
        carry = step(i - 1 - n, carry, False)
        return n + 1, least_decayed(carry), carry

    init = (jnp.zeros((tq, 1), F32), jnp.zeros((tq, LANES), F32))
    carry = step(i, (init, init), True)
    _, _, ((_, acc0), (_, acc1)) = lax.while_loop(cond, body, (jnp.int32(0), least_decayed(carry), carry))
    o_ref[...] = jnp.where(lane < HEAD_DIM, acc0, acc1).astype(BF16)


def _sb(q, k, v, B, S, tq):
    nq = S // tq
    npair = SB_HEADS // 2
    return pl.pallas_call(
        functools.partial(_sb_kernel, tq=tq),
        grid=(B, npair, nq),
        in_specs=[pl.BlockSpec((tq, LANES), lambda b, j, i: (b * nq + i, j)),
                  pl.BlockSpec((S, LANES), lambda b, j, i: (b, j)),
                  pl.BlockSpec((S, LANES), lambda b, j, i: (b, j))],
        out_specs=pl.BlockSpec((tq, LANES), lambda b, j, i: (b * nq + i, j)),
        out_shape=jax.ShapeDtypeStruct((B * S, SB_W), BF16),
        compiler_params=pltpu.CompilerParams(vmem_limit_bytes=VMEM_LIMIT),
        name="sb",
    )(q, k, v)


def _merge_kernel(x_ref, gn_ref, *rest):
    ng = len(DSA_GROUPS)
    oa_refs, lse_refs = rest[:ng], rest[ng:2 * ng]
    ob_ref, oc_ref, wg_ref, wb_ref, wo_ref, out_ref = rest[2 * ng:2 * ng + 6]
    scr = iter(rest[2 * ng + 6:])
    x = x_ref[...]
    tm = x.shape[0]
    h = _rms(x, gn_ref[...]).astype(BF16)

    def token_major(ref, d):
        if d == 1:
            return ref[0]
        buf = next(scr)
        nh = DSA_OUT // LANES
        for r in range(d):
            for half in range(nh):
                buf[half, pl.ds(r, tm // d, stride=d), :] = ref[r, :, half * LANES:(half + 1) * LANES]
        return jnp.concatenate([buf[half] for half in range(nh)], axis=-1)

    os_ = [token_major(oa_refs[g], d) for g, (_, d) in enumerate(DSA_GROUPS)]
    l0, l1, l2 = [token_major(lse_refs[g], d) for g, (_, d) in enumerate(DSA_GROUPS)]
    m = jnp.maximum(jnp.maximum(l0, l1), l2)
    e0, e1, e2 = jnp.exp(l0 - m), jnp.exp(l1 - m), jnp.exp(l2 - m)
    oa = ((e0 * os_[0] + e1 * os_[1] + e2 * os_[2]) / (e0 + e1 + e2)).astype(BF16)
    merged = None
    r0 = 0
    for b, o in enumerate((oa, ob_ref[...], oc_ref[...])):
        gate = jax.nn.sigmoid(_dot(h, wg_ref[:, b * D_MODEL:(b + 1) * D_MODEL]))
        term = gate * _dot(o, wb_ref[r0:r0 + o.shape[1], :])
        merged = term if merged is None else merged + term
        r0 += o.shape[1]
    out_ref[...] = x + _dot(merged.astype(BF16), wo_ref[...])


def _merge(x, p, oas, lses, ob, oc, S, tm):
    T = x.shape[0]
    ns = S // tm
    row = lambda w: pl.BlockSpec((tm, w), lambda i: (i, 0))
    grp = [pl.BlockSpec((None, d, tm // d, DSA_OUT), lambda i: (i // ns, 0, i % ns, 0)) for _, d in DSA_GROUPS]
    consts = [p['wg'], p['wb'], p['wo']]
    return pl.pallas_call(
        _merge_kernel,
        grid=(T // tm,),
        in_specs=[row(D_MODEL), _const_spec(p['gn'].shape)] + grp + grp + [row(MLA_OUT), row(SB_W)]
                 + [_const_spec(c.shape) for c in consts],
        out_specs=row(D_MODEL),
        out_shape=jax.ShapeDtypeStruct((T, D_MODEL), F32),
        scratch_shapes=[pltpu.VMEM((DSA_OUT // LANES, tm, LANES), F32)
                        for _ in range(2) for _, d in DSA_GROUPS if d > 1],
        compiler_params=pltpu.CompilerParams(vmem_limit_bytes=VMEM_LIMIT),
        name="merge",
    )(x, p['gn'], *oas, *lses, ob, oc, *consts)


def _mlp_kernel(x_ref, g_ref, w1_ref, w2_ref, out_ref, *, chunk):
    x = x_ref[...]
    h = _rms(x, g_ref[...]).astype(BF16)
    acc = x
    for c in range(D_FF // chunk):
        a = jnp.maximum(_dot(h, w1_ref[:, c * chunk:(c + 1) * chunk]), 0.0)
        acc = acc + _dot((a * a).astype(BF16), w2_ref[c * chunk:(c + 1) * chunk, :])
    out_ref[...] = acc


def _mlp(x, g, w1, w2, tm, chunk=512):
    T = x.shape[0]
    row = pl.BlockSpec((tm, D_MODEL), lambda i: (i, 0))
    return pl.pallas_call(
        functools.partial(_mlp_kernel, chunk=chunk),
        grid=(T // tm,),
        in_specs=[row, _const_spec(g.shape), _const_spec(w1.shape), _const_spec(w2.shape)],
        out_specs=row,
        out_shape=jax.ShapeDtypeStruct((T, D_MODEL), F32),
        compiler_params=pltpu.CompilerParams(vmem_limit_bytes=VMEM_LIMIT),
        name="mlp",
    )(x, g, w1, w2)


def _rope_lane_tables(S, dim, first_lane, period):
    half = dim // 2
    pos = jnp.arange(S, dtype=F32)
    inv_freq = ROPE_THETA ** (-jnp.arange(0, dim, 2, dtype=F32) / dim)
    ang = pos[:, None] * inv_freq[None, :]
    cos, sin = jnp.cos(ang), jnp.sin(ang)
    rel = jnp.arange(LANES) % period - first_lane
    idx = jnp.clip(rel, 0, dim - 1) % half
    in1 = (rel >= 0) & (rel < half)
    in2 = (rel >= half) & (rel < dim)
    c = jnp.where((in1 | in2)[None, :], cos[:, idx], 1.0)
    s1 = jnp.where(in2[None, :], sin[:, idx], 0.0)
    s2 = jnp.where(in1[None, :], -sin[:, idx], 0.0)
    return c, s1, s2


def _block_diag_ones(block):
    r = jnp.arange(MXU_DIM) // block
    return (r[:, None] == r[None, :]).astype(BF16)


def _prep_params(attn_norm, w_in, a_q_norm, a_k_norm, b_q_a_norm, w_q_b, b_kv_a_norm, w_kv_b,
                 b_q_norm, b_k_norm, w_branch, w_out, mlp_norm, w_ff1, w_ff2):
    L = w_in.shape[0]
    o_ql = 3 * DSA_W
    o_kr = o_ql + MLA_Q_LORA + MLA_KV_LORA
    o_c = o_kr + MLA_ROPE
    o_g = o_c + 3 * SB_W
    zeros = lambda n: jnp.zeros((L, D_MODEL, n), F32)
    ws = jnp.concatenate([w_in[:, :, o_ql:o_kr], zeros(MLA_NOPE), w_in[:, :, o_kr:o_c],
                          zeros(MLA_PAD - MLA_QK)], axis=-1)
    pad_head = lambda w: jnp.pad(w, ((0, 0), (0, 0), (0, 0), (0, MLA_PAD - w.shape[-1]))).reshape(
        L, w.shape[1], MLA_HEADS * MLA_PAD)
    pad_gain = lambda g: jnp.tile(jnp.pad(g, ((0, 0), (0, MLA_PAD - MLA_QK))), (1, MLA_HEADS))[:, None, :]
    n_heads_a = DSA_W // HEAD_DIM
    return {
        'gn': attn_norm[:, None, :],
        'wa': w_in[:, :, :o_ql].astype(BF16),
        'ws': ws.astype(BF16),
        'wc': w_in[:, :, o_c:o_g].astype(BF16),
        'wg': w_in[:, :, o_g:].astype(BF16),
        'wqb': pad_head(w_q_b).astype(BF16),
        'wk': pad_head(w_kv_b[..., :MLA_NOPE]).astype(BF16),
        'wv': w_kv_b[..., MLA_NOPE:].reshape(L, MLA_KV_LORA, MLA_OUT).astype(BF16),
        'gaq': jnp.tile(a_q_norm * HEAD_DIM ** -0.5, (1, n_heads_a))[:, None, :],
        'gak': jnp.tile(a_k_norm, (1, n_heads_a))[:, None, :],
        'gql': b_q_a_norm[:, None, :],
        'gkvl': b_kv_a_norm[:, None, :],
        'gbq': pad_gain(b_q_norm * (MLA_QK ** -0.5 * LOG2E)),
        'gbk': pad_gain(b_k_norm),
        'wb': w_branch.astype(BF16),
        'wo': w_out.astype(BF16),
        'gm': mlp_norm[:, None, :],
        'w1': w_ff1.astype(BF16),
        'w2': w_ff2.astype(BF16),
    }


def kernel(x, attn_norm, w_in, a_q_norm, a_k_norm, b_q_a_norm, w_q_b, b_kv_a_norm, w_kv_b, b_q_norm,
           b_k_norm, w_branch, w_out, mlp_norm, w_ff1, w_ff2):
    B, S, _ = x.shape
    depth = w_in.shape[0]
    params = _prep_params(attn_norm, w_in, a_q_norm, a_k_norm, b_q_a_norm, w_q_b, b_kv_a_norm, w_kv_b,
                          b_q_norm, b_k_norm, w_branch, w_out, mlp_norm, w_ff1, w_ff2)
    tabs = {
        'rope': _rope_lane_tables(S, ROT_DIM, 0, HEAD_DIM) + _rope_lane_tables(S, MLA_ROPE, MLA_NOPE, MLA_PAD),
        'm64': _block_diag_ones(HEAD_DIM),
        'm128': _block_diag_ones(MLA_PAD),
    }
    tm = min(512, S)
    xt = x.reshape(B * S, D_MODEL)
    for l in range(depth):
        p = {k: v[l] for k, v in params.items()}
        a_qkv, (bq, bk, bv, cq, ck, cv) = _proj(xt, p, tabs, S, tm)

        oas, lses = [], []
        ng = len(DSA_GROUPS)
        for g, (window, d) in enumerate(DSA_GROUPS):
            assert window // d == BLOCK and S % (BLOCK * d) == 0 and (B * S) % (2 * BLOCK) == 0
            qg, kg, vg = (a_qkv[t * ng + g].reshape(B * S, DSA_OUT) for t in range(3))
            o, lse = _dsa(qg, kg, vg, S // d // BLOCK)
            oas.append(o.reshape(B, d, S // d, DSA_OUT))
            lses.append(lse.reshape(B, d, S // d, DSA_OUT))
        ob = _mla(bq, bk, bv, B, S, min(MLA_TQ, S), min(MLA_TK, S))
        oc = _sb(cq, ck, cv, B, S, min(256, S))

        xt = _merge(xt, p, oas, lses, ob, oc, S, min(256, S))
        xt = _mlp(xt, p['gm'], p['w1'], p['w2'], tm)
    return xt.reshape(B, S, D_MODEL)
```

```python
import functools

import jax
import jax.numpy as jnp
from jax import lax
from jax.experimental import pallas as pl
from jax.experimental.pallas import tpu as pltpu

D_MODEL = 1024
HEAD_DIM = 64
ROPE_THETA = 500000.0
ROT_DIM = HEAD_DIM // 4
NORM_EPS = 1e-6
BLOCK = 128

DSA_GROUPS = ((128, 1), (512, 4), (2048, 16))
DSA_HEADS_PER_GROUP = 4
DSA_W = DSA_HEADS_PER_GROUP * len(DSA_GROUPS) * HEAD_DIM
DSA_OUT = DSA_HEADS_PER_GROUP * HEAD_DIM

MLA_HEADS = 8
MLA_Q_LORA = 256
MLA_KV_LORA = 128
MLA_NOPE = 64
MLA_ROPE = 32
MLA_QK = MLA_NOPE + MLA_ROPE
MLA_V = 64
MLA_OUT = MLA_HEADS * MLA_V
MLA_PAD = 128

SB_HEADS = 8
SB_W = SB_HEADS * HEAD_DIM

N_BRANCH = 3
D_FF = 4 * D_MODEL

LANES = 128
MXU_DIM = 256
VMEM_LIMIT = 56 * 1024 * 1024
MLA_TQ, MLA_TK = 512, 512
SB_TQ = 256

F32 = jnp.float32
BF16 = jnp.bfloat16
NEG = -1e30
LOG2E = 1.4426950408889634
SB_DEAD_LOG2 = -160.0


def _dot(a, b):
    return jnp.dot(a, b, preferred_element_type=F32)


def _dot_nt(a, b):
    return lax.dot_general(a, b, (((1,), (1,)), ((), ())), preferred_element_type=F32)


def _split_dot(s, m):
    hi = s.astype(BF16)
    lo = (s - hi.astype(F32)).astype(BF16)
    return _dot(hi, m) + _dot(lo, m)


def _rms(x, g):
    ms = jnp.mean(x * x, axis=-1, keepdims=True)
    return x * lax.rsqrt(ms + NORM_EPS) * g


def _head_norm(y, m, inv_n, g):
    ss = _split_dot(y * y, m)
    return y * lax.rsqrt(ss * inv_n + NORM_EPS) * g


def _rope(y, c, s1, s2, shift):
    outs = []
    for j in range(y.shape[1] // LANES):
        yj = y[:, j * LANES:(j + 1) * LANES]
        outs.append(yj * c + pltpu.roll(yj, shift, 1) * s1 + pltpu.roll(yj, LANES - shift, 1) * s2)
    return jnp.concatenate(outs, axis=-1)


def _proj_kernel(x_ref, gn_ref, wa_ref, ws_ref, wc_ref, wqb_ref, wk_ref, wv_ref,
                 gaq_ref, gak_ref, gql_ref, gkvl_ref, gbq_ref, gbk_ref,
                 ca_ref, s1a_ref, s2a_ref, cm_ref, s1m_ref, s2m_ref, m64_ref, m128_ref,
                 *rest):
    n_a = 3 * len(DSA_GROUPS)
    a_refs, (bq_ref, bk_ref, bv_ref, cq_ref, ck_ref, cv_ref) = rest[:n_a], rest[n_a:n_a + 6]
    scr_refs = rest[n_a + 6:]
    h = _rms(x_ref[...], gn_ref[...]).astype(BF16)
    ca, s1a, s2a = ca_ref[...], s1a_ref[...], s2a_ref[...]
    cm, s1m, s2m = cm_ref[...], s1m_ref[...], s2m_ref[...]
    m64, m128 = m64_ref[...], m128_ref[...]
    W = MXU_DIM
    tm = x_ref.shape[0]

    scr = iter(scr_refs)
    for t, g_ref in enumerate((gaq_ref, gak_ref, None)):
        for g, (_, d) in enumerate(DSA_GROUPS):
            y = _dot(h, wa_ref[:, t * DSA_W + g * W: t * DSA_W + (g + 1) * W])
            if g_ref is not None:
                y = _head_norm(y, m64, 1.0 / HEAD_DIM, g_ref[:, g * W:(g + 1) * W])
                y = _rope(y, ca, s1a, s2a, ROT_DIM // 2)
            dst = a_refs[t * len(DSA_GROUPS) + g]
            if d == 1:
                dst[0] = y.astype(BF16)
            else:
                buf = next(scr)
                for half in range(W // LANES):
                    buf[half] = y[:, half * LANES:(half + 1) * LANES]
                for r in range(d):
                    for half in range(W // LANES):
                        dst[r, :, half * LANES:(half + 1) * LANES] = (
                            buf[half, pl.ds(r, tm // d, stride=d), :].astype(BF16))

    ys = _dot(h, ws_ref[...])
    ql = _rms(ys[:, :MLA_Q_LORA], gql_ref[...]).astype(BF16)
    kvl = _rms(ys[:, MLA_Q_LORA:MLA_Q_LORA + MLA_KV_LORA], gkvl_ref[...]).astype(BF16)
    kr = ys[:, MLA_Q_LORA + MLA_KV_LORA:]
    kr2 = jnp.concatenate([kr, kr], axis=-1)
    for c in range(MLA_HEADS * MLA_PAD // W):
        yq = _dot(ql, wqb_ref[:, c * W:(c + 1) * W])
        yq = _head_norm(yq, m128, 1.0 / MLA_QK, gbq_ref[:, c * W:(c + 1) * W])
        bq_ref[:, c * W:(c + 1) * W] = _rope(yq, cm, s1m, s2m, MLA_ROPE // 2).astype(BF16)
        yk = _dot(kvl, wk_ref[:, c * W:(c + 1) * W]) + kr2
        yk = _head_norm(yk, m128, 1.0 / MLA_QK, gbk_ref[:, c * W:(c + 1) * W])
        bk_ref[:, c * W:(c + 1) * W] = _rope(yk, cm, s1m, s2m, MLA_ROPE // 2).astype(BF16)
    bv_ref[...] = _dot(kvl, wv_ref[...]).astype(BF16)

    cq_ref[...] = (_dot(h, wc_ref[:, :SB_W]) * (HEAD_DIM ** -0.5 * LOG2E)).astype(BF16)
    ck_ref[...] = _dot(h, wc_ref[:, SB_W:2 * SB_W]).astype(BF16)
    cv_ref[...] = _dot(h, wc_ref[:, 2 * SB_W:]).astype(BF16)


def _const_spec(shape):
    nd = len(shape)
    return pl.BlockSpec(shape, lambda *_: (0,) * nd)


def _proj(x, p, tabs, S, tm):
    T = x.shape[0]
    ns = S // tm
    row = lambda w: pl.BlockSpec((tm, w), lambda i: (i, 0))
    tab = pl.BlockSpec((tm, LANES), lambda i: (i % ns, 0))
    consts = [p['gn'], p['wa'], p['ws'], p['wc'], p['wqb'], p['wk'], p['wv'],
              p['gaq'], p['gak'], p['gql'], p['gkvl'], p['gbq'], p['gbk']]
    ins = [x] + consts + list(tabs['rope']) + [tabs['m64'], tabs['m128']]
    in_specs = ([row(D_MODEL)] + [_const_spec(c.shape) for c in consts] + [tab] * 6
                + [_const_spec((MXU_DIM, MXU_DIM))] * 2)
    widths = (MLA_HEADS * MLA_PAD, MLA_HEADS * MLA_PAD, MLA_OUT, SB_W, SB_W, SB_W)
    B = T // S
    dils = [d for _ in range(3) for _, d in DSA_GROUPS]
    a_specs = [pl.BlockSpec((None, d, tm // d, DSA_OUT), lambda i: (i // ns, 0, i % ns, 0)) for d in dils]
    a_shapes = [jax.ShapeDtypeStruct((B, d, S // d, DSA_OUT), BF16) for d in dils]
    outs = pl.pallas_call(
        _proj_kernel,
        grid=(T // tm,),
        in_specs=in_specs,
        out_specs=a_specs + [row(w) for w in widths],
        out_shape=a_shapes + [jax.ShapeDtypeStruct((T, w), BF16) for w in widths],
        scratch_shapes=[pltpu.VMEM((DSA_OUT // LANES, tm, LANES), F32) for d in dils if d > 1],
        compiler_params=pltpu.CompilerParams(vmem_limit_bytes=VMEM_LIMIT),
        name="proj",
    )(*ins)
    return outs[:len(dils)], outs[len(dils):]


def _dsa_kernel(q_ref, kp_ref, kc_ref, vp_ref, vc_ref, o_ref, lse_ref, *, nb_seq):
    n = pl.program_id(0)
    pens = [jnp.where((2 * n + blk) % nb_seq == 0, NEG, 0.0).astype(F32) for blk in range(2)]
    qi = lax.broadcasted_iota(jnp.int32, (BLOCK, 2 * BLOCK), 0)
    kj = lax.broadcasted_iota(jnp.int32, (BLOCK, 2 * BLOCK), 1)
    band = (kj >= qi) & (kj <= qi + BLOCK)
    lane = lax.broadcasted_iota(jnp.int32, (BLOCK, LANES), 1)
    for blk in range(2):
        rows = slice(blk * BLOCK, (blk + 1) * BLOCK)
        pen_row = jnp.where(kj < BLOCK, pens[blk], 0.0)
        for pr in range(DSA_OUT // LANES):
            sl = slice(pr * LANES, (pr + 1) * LANES)
            q = q_ref[rows, sl]
            if blk == 0:
                k2 = jnp.concatenate([kp_ref[:, sl], kc_ref[:BLOCK, sl]], axis=0)
                v2 = jnp.concatenate([vp_ref[:, sl], vc_ref[:BLOCK, sl]], axis=0)
            else:
                k2, v2 = kc_ref[:, sl], vc_ref[:, sl]
            o_acc = l_acc = None
            for hh in range(2):
                sel = (lane < HEAD_DIM) if hh == 0 else (lane >= HEAD_DIM)
                qh = jnp.where(sel, q, jnp.zeros_like(q))
                s = jnp.where(band, _dot_nt(qh, k2) + pen_row, NEG)
                m = jnp.max(s, axis=-1, keepdims=True)
                p = jnp.exp(s - m)
                den = jnp.sum(p, axis=-1, keepdims=True)
                o = _dot(p.astype(BF16), v2) / den
                lse = m + jnp.log(den)
                o_acc = o if hh == 0 else jnp.where(sel, o, o_acc)
                l_acc = jnp.broadcast_to(lse, o.shape) if hh == 0 else jnp.where(sel, lse, l_acc)
            o_ref[rows, sl] = o_acc
            lse_ref[rows, sl] = l_acc


def _dsa(q, k, v, nb_seq):
    T = q.shape[0]
    cur = pl.BlockSpec((2 * BLOCK, DSA_OUT), lambda n: (n, 0))
    prev = pl.BlockSpec((BLOCK, DSA_OUT), lambda n: (jnp.maximum(2 * n - 1, 0), 0))
    return pl.pallas_call(
        functools.partial(_dsa_kernel, nb_seq=nb_seq),
        grid=(T // (2 * BLOCK),),
        in_specs=[cur, prev, cur, prev, cur],
        out_specs=[cur, cur],
        out_shape=[jax.ShapeDtypeStruct((T, DSA_OUT), F32)] * 2,
        name="dsa",
    )(q, k, k, v, v)


def _mla_kernel(q_ref, k_ref, v_ref, o_ref, *, tq, tk):
    i = pl.program_id(2)
    krow = lax.broadcasted_iota(jnp.int32, (tk, 2 * tq), 0)
    qcol = lax.broadcasted_iota(jnp.int32, (tk, 2 * tq), 1)
    qidx = jnp.where(qcol >= tq, qcol - tq, qcol)
    qs = [q_ref[:, hh * MLA_PAD:(hh + 1) * MLA_PAD] for hh in range(2)]
    ndiag = tq // tk

    def step(kb, carry, diag):
        m, l, acc = carry
        start = pl.multiple_of(kb * tk, tk)
        s = jnp.concatenate([_dot_nt(k_ref[pl.ds(start, tk), hh * MLA_PAD:(hh + 1) * MLA_PAD], qs[hh])
                             for hh in range(2)], axis=-1)
        if diag is not None:
            s = jnp.where(krow + diag * tk <= qidx, s, NEG)
        m_new = jnp.maximum(m, jnp.max(s, axis=0, keepdims=True))
        alpha = jnp.exp2(m - m_new)
        p = jnp.exp2(s - m_new)
        l = l * alpha + jnp.sum(p, axis=0, keepdims=True)
        acc = acc * alpha + lax.dot_general(v_ref[pl.ds(start, tk), :], p.astype(BF16),
                                            (((0,), (0,)), ((), ())), preferred_element_type=F32)
        return m_new, l, acc

    carry = (jnp.full((1, 2 * tq), NEG, F32), jnp.zeros((1, 2 * tq), F32), jnp.zeros((LANES, 2 * tq), F32))
    carry = lax.fori_loop(0, i * ndiag, lambda kb, c: step(kb, c, None), carry)
    for j in range(ndiag):
        carry = step(i * ndiag + j, carry, j)
    _, l, acc = carry
    out = acc / l
    dim = lax.broadcasted_iota(jnp.int32, (LANES, tq), 0)
    o_ref[...] = jnp.where(dim < MLA_V, out[:, :tq], out[:, tq:]).T.astype(BF16)


def _mla(q, k, v, B, S, tq, tk):
    nq = S // tq
    npair = MLA_HEADS // 2
    return pl.pallas_call(
        functools.partial(_mla_kernel, tq=tq, tk=tk),
        grid=(B, npair, nq),
        in_specs=[pl.BlockSpec((tq, 2 * MLA_PAD), lambda b, j, i: (b * nq + i, j)),
                  pl.BlockSpec((S, 2 * MLA_PAD), lambda b, j, i: (b, j)),
                  pl.BlockSpec((S, LANES), lambda b, j, i: (b, j))],
        out_specs=pl.BlockSpec((tq, LANES), lambda b, j, i: (b * nq + i, j)),
        out_shape=jax.ShapeDtypeStruct((B * S, MLA_OUT), BF16),
        compiler_params=pltpu.CompilerParams(vmem_limit_bytes=VMEM_LIMIT),
        name="mla",
    )(q, k, v)


def _sb_kernel(q_ref, k_ref, v_ref, o_ref, *, tq):
    i = pl.program_id(2)
    krow = lax.broadcasted_iota(jnp.int32, (tq, 2 * tq), 0)
    qcol = lax.broadcasted_iota(jnp.int32, (tq, 2 * tq), 1)
    strict = krow < jnp.where(qcol >= tq, qcol - tq, qcol)
    r = lax.broadcasted_iota(jnp.int32, (tq, tq), 0)
    c = lax.broadcasted_iota(jnp.int32, (tq, tq), 1)
    later = jnp.where(c > r, 1.0, 0.0).astype(BF16)
    lane = lax.broadcasted_iota(jnp.int32, (tq, LANES), 1)
    q = q_ref[...]
    q2 = jnp.concatenate([jnp.where(lane < HEAD_DIM, q, jnp.zeros_like(q)),
                          jnp.where(lane >= HEAD_DIM, q, jnp.zeros_like(q))], axis=0)

    def step(kb, carry, masked):
        after, acc = carry
        start = pl.multiple_of(kb * tq, tq)
        z = _dot_nt(k_ref[pl.ds(start, tq), :], q2)
        log_beta = jnp.minimum(z, 0.0) - jnp.log(1.0 + jnp.exp2(-jnp.abs(z))) * LOG2E
        lom = log_beta - z
        if masked:
            lom = jnp.where(strict, lom, 0.0)
        hi = lom.astype(BF16)
        lo = (lom - hi.astype(F32)).astype(BF16)
        w = jnp.exp2(log_beta + (_dot(later, hi) + _dot(later, lo)) + after)
        if masked:
            w = jnp.where(strict, w, 0.0)
        acc = acc + lax.dot_general(v_ref[pl.ds(start, tq), :], w.astype(BF16), (((0,), (0,)), ((), ())),
                                    preferred_element_type=F32)
        after = after + jnp.sum(lom, axis=0, keepdims=True)
        return after, acc

    def cond(state):
        n, top, _ = state
        return jnp.logical_and(n < i, top > SB_DEAD_LOG2)

    def body(state):
        n, _, carry = state
        carry = step(i - 1 - n, carry, False)
        return n + 1, jnp.max(carry[0]), carry

    carry = step(i, (jnp.zeros((1, 2 * tq), F32), jnp.zeros((LANES, 2 * tq), F32)), True)
    _, _, (_, acc) = lax.while_loop(cond, body, (jnp.int32(0), jnp.max(carry[0]), carry))
    dim = lax.broadcasted_iota(jnp.int32, (LANES, tq), 0)
    o_ref[...] = jnp.where(dim < HEAD_DIM, acc[:, :tq], acc[:, tq:]).T.astype(BF16)


def _sb(q, k, v, B, S, tq):
    nq = S // tq
    w = LANES
    return pl.pallas_call(
        functools.partial(_sb_kernel, tq=tq),
        grid=(B, SB_W // w, nq),
        in_specs=[pl.BlockSpec((tq, w), lambda b, j, i: (b * nq + i, j)),
                  pl.BlockSpec((S, w), lambda b, j, i: (b, j)),
                  pl.BlockSpec((S, w), lambda b, j, i: (b, j))],
        out_specs=pl.BlockSpec((tq, w), lambda b, j, i: (b * nq + i, j)),
        out_shape=jax.ShapeDtypeStruct((B * S, SB_W), BF16),
        compiler_params=pltpu.CompilerParams(vmem_limit_bytes=VMEM_LIMIT),
        name="sb",
    )(q, k, v)


def _merge_kernel(x_ref, gn_ref, *rest):
    ng = len(DSA_GROUPS)
    oa_refs, lse_refs = rest[:ng], rest[ng:2 * ng]
    ob_ref, oc_ref, wg_ref, wb_ref, wo_ref, out_ref = rest[2 * ng:2 * ng + 6]
    scr = iter(rest[2 * ng + 6:])
    x = x_ref[...]
    tm = x.shape[0]
    h = _rms(x, gn_ref[...]).astype(BF16)

    def token_major(ref, d):
        if d == 1:
            return ref[0]
        buf = next(scr)
        nh = DSA_OUT // LANES
        for r in range(d):
            for half in range(nh):
                buf[half, pl.ds(r, tm // d, stride=d), :] = ref[r, :, half * LANES:(half + 1) * LANES]
        return jnp.concatenate([buf[half] for half in range(nh)], axis=-1)

    os_ = [token_major(oa_refs[g], d) for g, (_, d) in enumerate(DSA_GROUPS)]
    l0, l1, l2 = [token_major(lse_refs[g], d) for g, (_, d) in enumerate(DSA_GROUPS)]
    m = jnp.maximum(jnp.maximum(l0, l1), l2)
    e0, e1, e2 = jnp.exp(l0 - m), jnp.exp(l1 - m), jnp.exp(l2 - m)
    oa = ((e0 * os_[0] + e1 * os_[1] + e2 * os_[2]) / (e0 + e1 + e2)).astype(BF16)
    merged = None
    r0 = 0
    for b, o in enumerate((oa, ob_ref[...], oc_ref[...])):
        gate = jax.nn.sigmoid(_dot(h, wg_ref[:, b * D_MODEL:(b + 1) * D_MODEL]))
        term = gate * _dot(o, wb_ref[r0:r0 + o.shape[1], :])
        merged = term if merged is None else merged + term
        r0 += o.shape[1]
    out_ref[...] = x + _dot(merged.astype(BF16), wo_ref[...])


def _merge(x, p, oas, lses, ob, oc, S, tm):
    T = x.shape[0]
    ns = S // tm
    row = lambda w: pl.BlockSpec((tm, w), lambda i: (i, 0))
    grp = [pl.BlockSpec((None, d, tm // d, DSA_OUT), lambda i: (i // ns, 0, i % ns, 0)) for _, d in DSA_GROUPS]
    consts = [p['wg'], p['wb'], p['wo']]
    return pl.pallas_call(
        _merge_kernel,
        grid=(T // tm,),
        in_specs=[row(D_MODEL), _const_spec(p['gn'].shape)] + grp + grp + [row(MLA_OUT), row(SB_W)]
                 + [_const_spec(c.shape) for c in consts],
        out_specs=row(D_MODEL),
        out_shape=jax.ShapeDtypeStruct((T, D_MODEL), F32),
        scratch_shapes=[pltpu.VMEM((DSA_OUT // LANES, tm, LANES), F32)
                        for _ in range(2) for _, d in DSA_GROUPS if d > 1],
        compiler_params=pltpu.CompilerParams(vmem_limit_bytes=VMEM_LIMIT),
        name="merge",
    )(x, p['gn'], *oas, *lses, ob, oc, *consts)


def _mlp_kernel(x_ref, g_ref, w1_ref, w2_ref, out_ref, *, chunk):
    x = x_ref[...]
    h = _rms(x, g_ref[...]).astype(BF16)
    acc = x
    for c in range(D_FF // chunk):
        a = jnp.maximum(_dot(h, w1_ref[:, c * chunk:(c + 1) * chunk]), 0.0)
        acc = acc + _dot((a * a).astype(BF16), w2_ref[c * chunk:(c + 1) * chunk, :])
    out_ref[...] = acc


def _mlp(x, g, w1, w2, tm, chunk=512):
    T = x.shape[0]
    row = pl.BlockSpec((tm, D_MODEL), lambda i: (i, 0))
    return pl.pallas_call(
        functools.partial(_mlp_kernel, chunk=chunk),
        grid=(T // tm,),
        in_specs=[row, _const_spec(g.shape), _const_spec(w1.shape), _const_spec(w2.shape)],
        out_specs=row,
        out_shape=jax.ShapeDtypeStruct((T, D_MODEL), F32),
        compiler_params=pltpu.CompilerParams(vmem_limit_bytes=VMEM_LIMIT),
        name="mlp",
    )(x, g, w1, w2)


def _rope_lane_tables(S, dim, first_lane, period):
    half = dim // 2
    pos = jnp.arange(S, dtype=F32)
    inv_freq = ROPE_THETA ** (-jnp.arange(0, dim, 2, dtype=F32) / dim)
    ang = pos[:, None] * inv_freq[None, :]
    cos, sin = jnp.cos(ang), jnp.sin(ang)
    rel = jnp.arange(LANES) % period - first_lane
    idx = jnp.clip(rel, 0, dim - 1) % half
    in1 = (rel >= 0) & (rel < half)
    in2 = (rel >= half) & (rel < dim)
    c = jnp.where((in1 | in2)[None, :], cos[:, idx], 1.0)
    s1 = jnp.where(in2[None, :], sin[:, idx], 0.0)
    s2 = jnp.where(in1[None, :], -sin[:, idx], 0.0)
    return c, s1, s2


def _block_diag_ones(block):
    r = jnp.arange(MXU_DIM) // block
    return (r[:, None] == r[None, :]).astype(BF16)


def _prep_params(attn_norm, w_in, a_q_norm, a_k_norm, b_q_a_norm, w_q_b, b_kv_a_norm, w_kv_b,
                 b_q_norm, b_k_norm, w_branch, w_out, mlp_norm, w_ff1, w_ff2):
    L = w_in.shape[0]
    o_ql = 3 * DSA_W
    o_kr = o_ql + MLA_Q_LORA + MLA_KV_LORA
    o_c = o_kr + MLA_ROPE
    o_g = o_c + 3 * SB_W
    zeros = lambda n: jnp.zeros((L, D_MODEL, n), F32)
    ws = jnp.concatenate([w_in[:, :, o_ql:o_kr], zeros(MLA_NOPE), w_in[:, :, o_kr:o_c],
                          zeros(MLA_PAD - MLA_QK)], axis=-1)
    pad_head = lambda w: jnp.pad(w, ((0, 0), (0, 0), (0, 0), (0, MLA_PAD - w.shape[-1]))).reshape(
        L, w.shape[1], MLA_HEADS * MLA_PAD)
    pad_gain = lambda g: jnp.tile(jnp.pad(g, ((0, 0), (0, MLA_PAD - MLA_QK))), (1, MLA_HEADS))[:, None, :]
    n_heads_a = DSA_W // HEAD_DIM
    return {
        'gn': attn_norm[:, None, :],
        'wa': w_in[:, :, :o_ql].astype(BF16),
        'ws': ws.astype(BF16),
        'wc': w_in[:, :, o_c:o_g].astype(BF16),
        'wg': w_in[:, :, o_g:].astype(BF16),
        'wqb': pad_head(w_q_b).astype(BF16),
        'wk': pad_head(w_kv_b[..., :MLA_NOPE]).astype(BF16),
        'wv': w_kv_b[..., MLA_NOPE:].reshape(L, MLA_KV_LORA, MLA_OUT).astype(BF16),
        'gaq': jnp.tile(a_q_norm * HEAD_DIM ** -0.5, (1, n_heads_a))[:, None, :],
        'gak': jnp.tile(a_k_norm, (1, n_heads_a))[:, None, :],
        'gql': b_q_a_norm[:, None, :],
        'gkvl': b_kv_a_norm[:, None, :],
        'gbq': pad_gain(b_q_norm * (MLA_QK ** -0.5 * LOG2E)),
        'gbk': pad_gain(b_k_norm),
        'wb': w_branch.astype(BF16),
        'wo': w_out.astype(BF16),
        'gm': mlp_norm[:, None, :],
        'w1': w_ff1.astype(BF16),
        'w2': w_ff2.astype(BF16),
    }


def kernel(x, attn_norm, w_in, a_q_norm, a_k_norm, b_q_a_norm, w_q_b, b_kv_a_norm, w_kv_b, b_q_norm,
           b_k_norm, w_branch, w_out, mlp_norm, w_ff1, w_ff2):
    B, S, _ = x.shape
    depth = w_in.shape[0]
    params = _prep_params(attn_norm, w_in, a_q_norm, a_k_norm, b_q_a_norm, w_q_b, b_kv_a_norm, w_kv_b,
                          b_q_norm, b_k_norm, w_branch, w_out, mlp_norm, w_ff1, w_ff2)
    tabs = {
        'rope': _rope_lane_tables(S, ROT_DIM, 0, HEAD_DIM) + _rope_lane_tables(S, MLA_ROPE, MLA_NOPE, MLA_PAD),
        'm64': _block_diag_ones(HEAD_DIM),
        'm128': _block_diag_ones(MLA_PAD),
    }
    tm = min(512, S)
    xt = x.reshape(B * S, D_MODEL)
    for l in range(depth):
        p = {k: v[l] for k, v in params.items()}
        a_qkv, (bq, bk, bv, cq, ck, cv) = _proj(xt, p, tabs, S, tm)

        oas, lses = [], []
        ng = len(DSA_GROUPS)
        for g, (window, d) in enumerate(DSA_GROUPS):
            assert window // d == BLOCK and S % (BLOCK * d) == 0 and (B * S) % (2 * BLOCK) == 0
            qg, kg, vg = (a_qkv[t * ng + g].reshape(B * S, DSA_OUT) for t in range(3))
            o, lse = _dsa(qg, kg, vg, S // d // BLOCK)
            oas.append(o.reshape(B, d, S // d, DSA_OUT))
            lses.append(lse.reshape(B, d, S // d, DSA_OUT))
        ob = _mla(bq, bk, bv, B, S, min(MLA_TQ, S), min(MLA_TK, S))
        oc = _sb(cq, ck, cv, B, S, min(SB_TQ, S))

        xt = _merge(xt, p, oas, lses, ob, oc, S, min(256, S))
        xt = _mlp(xt, p['gm'], p['w1'], p['w2'], tm)
    return xt.reshape(B, S, D_MODEL)
```

```python
import functools

import jax
import jax.numpy as jnp
from jax import lax
from jax.experimental import pallas as pl
from jax.experimental.pallas import tpu as pltpu

D_MODEL = 1024
HEAD_DIM = 64
ROPE_THETA = 500000.0
ROT_DIM = HEAD_DIM // 4
NORM_EPS = 1e-6
BLOCK = 128

DSA_GROUPS = ((128, 1), (512, 4), (2048, 16))
DSA_HEADS_PER_GROUP = 4
DSA_W = DSA_HEADS_PER_GROUP * len(DSA_GROUPS) * HEAD_DIM
DSA_OUT = DSA_HEADS_PER_GROUP * HEAD_DIM

MLA_HEADS = 8
MLA_Q_LORA = 256
MLA_KV_LORA = 128
MLA_NOPE = 64
MLA_ROPE = 32
MLA_QK = MLA_NOPE + MLA_ROPE
MLA_V = 64
MLA_OUT = MLA_HEADS * MLA_V
MLA_PAD = 128

SB_HEADS = 8
SB_W = SB_HEADS * HEAD_DIM

N_BRANCH = 3
D_FF = 4 * D_MODEL

LANES = 128
MXU_DIM = 256
VMEM_LIMIT = 56 * 1024 * 1024
MLA_TQ, MLA_TK = 512, 512
SB_TQ = 256
DSA_BLOCKS_PER_STEP = 4

F32 = jnp.float32
BF16 = jnp.bfloat16
NEG = -1e30
LOG2E = 1.4426950408889634
SB_DEAD_LOG2 = -160.0


def _dot(a, b):
    return jnp.dot(a, b, preferred_element_type=F32)


def _dot_nt(a, b):
    return lax.dot_general(a, b, (((1,), (1,)), ((), ())), preferred_element_type=F32)


def _split_dot(s, m):
    hi = s.astype(BF16)
    lo = (s - hi.astype(F32)).astype(BF16)
    return _dot(hi, m) + _dot(lo, m)


def _rms(x, g):
    ms = jnp.mean(x * x, axis=-1, keepdims=True)
    return x * lax.rsqrt(ms + NORM_EPS) * g


def _head_norm(y, m, inv_n, g):
    ss = _split_dot(y * y, m)
    return y * lax.rsqrt(ss * inv_n + NORM_EPS) * g


def _rope(y, c, s1, s2, shift):
    outs = []
    for j in range(y.shape[1] // LANES):
        yj = y[:, j * LANES:(j + 1) * LANES]
        outs.append(yj * c + pltpu.roll(yj, shift, 1) * s1 + pltpu.roll(yj, LANES - shift, 1) * s2)
    return jnp.concatenate(outs, axis=-1)


def _proj_kernel(x_ref, gn_ref, wa_ref, ws_ref, wc_ref, wqb_ref, wk_ref, wv_ref,
                 gaq_ref, gak_ref, gql_ref, gkvl_ref, gbq_ref, gbk_ref,
                 ca_ref, s1a_ref, s2a_ref, cm_ref, s1m_ref, s2m_ref, m64_ref, m128_ref,
                 *rest):
    n_a = 3 * len(DSA_GROUPS)
    a_refs, (bq_ref, bk_ref, bv_ref, cq_ref, ck_ref, cv_ref) = rest[:n_a], rest[n_a:n_a + 6]
    scr_refs = rest[n_a + 6:]
    h = _rms(x_ref[...], gn_ref[...]).astype(BF16)
    ca, s1a, s2a = ca_ref[...], s1a_ref[...], s2a_ref[...]
    cm, s1m, s2m = cm_ref[...], s1m_ref[...], s2m_ref[...]
    m64, m128 = m64_ref[...], m128_ref[...]
    W = MXU_DIM
    tm = x_ref.shape[0]

    jobs = []
    low = {}

    def low_rank(ys):
        low['ql'] = _rms(ys[:, :MLA_Q_LORA], gql_ref[...]).astype(BF16)
        low['kvl'] = _rms(ys[:, MLA_Q_LORA:MLA_Q_LORA + MLA_KV_LORA], gkvl_ref[...]).astype(BF16)
        kr = ys[:, MLA_Q_LORA + MLA_KV_LORA:]
        low['kr2'] = jnp.concatenate([kr, kr], axis=-1)
    jobs.append((lambda: _dot(h, ws_ref[...]), low_rank))

    scr = iter(scr_refs)
    for t, g_ref in enumerate((gaq_ref, gak_ref, None)):
        for g, (_, d) in enumerate(DSA_GROUPS):
            def a_epilogue(y, g_ref=g_ref, g=g, d=d, dst=a_refs[t * len(DSA_GROUPS) + g],
                           buf=next(scr) if d > 1 else None):
                if g_ref is not None:
                    y = _head_norm(y, m64, 1.0 / HEAD_DIM, g_ref[:, g * W:(g + 1) * W])
                    y = _rope(y, ca, s1a, s2a, ROT_DIM // 2)
                if d == 1:
                    dst[0] = y.astype(BF16)
                    return
                for half in range(W // LANES):
                    buf[half] = y[:, half * LANES:(half + 1) * LANES]
                for r in range(d):
                    for half in range(W // LANES):
                        dst[r, :, half * LANES:(half + 1) * LANES] = (
                            buf[half, pl.ds(r, tm // d, stride=d), :].astype(BF16))
            col = t * DSA_W + g * W
            jobs.append((lambda col=col: _dot(h, wa_ref[:, col:col + W]), a_epilogue))

    for n, (dst, scale) in enumerate(((cq_ref, HEAD_DIM ** -0.5 * LOG2E), (ck_ref, None), (cv_ref, None))):
        def c_epilogue(y, dst=dst, scale=scale):
            dst[...] = (y if scale is None else y * scale).astype(BF16)
        jobs.append((lambda n=n: _dot(h, wc_ref[:, n * SB_W:(n + 1) * SB_W]), c_epilogue))

    for c in range(MLA_HEADS * MLA_PAD // W):
        cols = slice(c * W, (c + 1) * W)

        def q_epilogue(y, cols=cols):
            y = _head_norm(y, m128, 1.0 / MLA_QK, gbq_ref[:, cols])
            bq_ref[:, cols] = _rope(y, cm, s1m, s2m, MLA_ROPE // 2).astype(BF16)

        def k_epilogue(y, cols=cols):
            y = _head_norm(y + low['kr2'], m128, 1.0 / MLA_QK, gbk_ref[:, cols])
            bk_ref[:, cols] = _rope(y, cm, s1m, s2m, MLA_ROPE // 2).astype(BF16)

        jobs.append((lambda cols=cols: _dot(low['ql'], wqb_ref[:, cols]), q_epilogue))
        jobs.append((lambda cols=cols: _dot(low['kvl'], wk_ref[:, cols]), k_epilogue))

    def v_epilogue(y):
        bv_ref[...] = y.astype(BF16)
    jobs.append((lambda: _dot(low['kvl'], wv_ref[...]), v_epilogue))

    y = jobs[0][0]()
    for j, (_, epilogue) in enumerate(jobs):
        y_next = jobs[j + 1][0]() if j + 1 < len(jobs) else None
        epilogue(y)
        y = y_next


def _const_spec(shape):
    nd = len(shape)
    return pl.BlockSpec(shape, lambda *_: (0,) * nd, pipeline_mode=pl.Buffered(1))


def _proj(x, p, tabs, S, tm):
    T = x.shape[0]
    ns = S // tm
    row = lambda w: pl.BlockSpec((tm, w), lambda i: (i, 0))
    tab = pl.BlockSpec((tm, LANES), lambda i: (i % ns, 0))
    consts = [p['gn'], p['wa'], p['ws'], p['wc'], p['wqb'], p['wk'], p['wv'],
              p['gaq'], p['gak'], p['gql'], p['gkvl'], p['gbq'], p['gbk']]
    ins = [x] + consts + list(tabs['rope']) + [tabs['m64'], tabs['m128']]
    in_specs = ([row(D_MODEL)] + [_const_spec(c.shape) for c in consts] + [tab] * 6
                + [_const_spec((MXU_DIM, MXU_DIM))] * 2)
    widths = (MLA_HEADS * MLA_PAD, MLA_HEADS * MLA_PAD, MLA_OUT, SB_W, SB_W, SB_W)
    B = T // S
    dils = [d for _ in range(3) for _, d in DSA_GROUPS]
    a_specs = [pl.BlockSpec((None, d, tm // d, DSA_OUT), lambda i: (i // ns, 0, i % ns, 0)) for d in dils]
    a_shapes = [jax.ShapeDtypeStruct((B, d, S // d, DSA_OUT), BF16) for d in dils]
    outs = pl.pallas_call(
        _proj_kernel,
        grid=(T // tm,),
        in_specs=in_specs,
        out_specs=a_specs + [row(w) for w in widths],
        out_shape=a_shapes + [jax.ShapeDtypeStruct((T, w), BF16) for w in widths],
        scratch_shapes=[pltpu.VMEM((DSA_OUT // LANES, tm, LANES), F32) for d in dils if d > 1],
        compiler_params=pltpu.CompilerParams(vmem_limit_bytes=VMEM_LIMIT),
        name="proj",
    )(*ins)
    return outs[:len(dils)], outs[len(dils):]


def _dsa_kernel(q_ref, kp_ref, kc_ref, vp_ref, vc_ref, o_ref, lse_ref, *, nb_seq):
    n = pl.program_id(0)
    nblk = DSA_BLOCKS_PER_STEP
    pens = [jnp.where((nblk * n + blk) % nb_seq == 0, NEG, 0.0).astype(F32) for blk in range(nblk)]
    qi = lax.broadcasted_iota(jnp.int32, (BLOCK, 2 * BLOCK), 0)
    kj = lax.broadcasted_iota(jnp.int32, (BLOCK, 2 * BLOCK), 1)
    band = (kj >= qi) & (kj <= qi + BLOCK)
    lane = lax.broadcasted_iota(jnp.int32, (BLOCK, LANES), 1)
    for blk in range(nblk):
        rows = slice(blk * BLOCK, (blk + 1) * BLOCK)
        keys = slice((blk - 1) * BLOCK, (blk + 1) * BLOCK)
        pen_row = jnp.where(kj < BLOCK, pens[blk], 0.0)
        for pr in range(DSA_OUT // LANES):
            sl = slice(pr * LANES, (pr + 1) * LANES)
            q = q_ref[rows, sl]
            if blk == 0:
                k2 = jnp.concatenate([kp_ref[:, sl], kc_ref[:BLOCK, sl]], axis=0)
                v2 = jnp.concatenate([vp_ref[:, sl], vc_ref[:BLOCK, sl]], axis=0)
            else:
                k2, v2 = kc_ref[keys, sl], vc_ref[keys, sl]
            o_acc = l_acc = None
            for hh in range(2):
                sel = (lane < HEAD_DIM) if hh == 0 else (lane >= HEAD_DIM)
                qh = jnp.where(sel, q, jnp.zeros_like(q))
                s = jnp.where(band, _dot_nt(qh, k2) + pen_row, NEG)
                m = jnp.max(s, axis=-1, keepdims=True)
                p = jnp.exp(s - m)
                den = jnp.sum(p, axis=-1, keepdims=True)
                o = _dot(p.astype(BF16), v2) / den
                lse = m + jnp.log(den)
                o_acc = o if hh == 0 else jnp.where(sel, o, o_acc)
                l_acc = jnp.broadcast_to(lse, o.shape) if hh == 0 else jnp.where(sel, lse, l_acc)
            o_ref[rows, sl] = o_acc
            lse_ref[rows, sl] = l_acc


def _dsa(q, k, v, nb_seq):
    T = q.shape[0]
    nblk = DSA_BLOCKS_PER_STEP
    cur = pl.BlockSpec((nblk * BLOCK, DSA_OUT), lambda n: (n, 0))
    prev = pl.BlockSpec((BLOCK, DSA_OUT), lambda n: (jnp.maximum(nblk * n - 1, 0), 0))
    return pl.pallas_call(
        functools.partial(_dsa_kernel, nb_seq=nb_seq),
        grid=(T // (nblk * BLOCK),),
        in_specs=[cur, prev, cur, prev, cur],
        out_specs=[cur, cur],
        out_shape=[jax.ShapeDtypeStruct((T, DSA_OUT), F32)] * 2,
        name="dsa",
    )(q, k, k, v, v)


def _mla_kernel(q_ref, k_ref, v_ref, o_ref, *, tq, tk):
    i = pl.program_id(2)
    krow = lax.broadcasted_iota(jnp.int32, (tk, 2 * tq), 0)
    qcol = lax.broadcasted_iota(jnp.int32, (tk, 2 * tq), 1)
    qidx = jnp.where(qcol >= tq, qcol - tq, qcol)
    qs = [q_ref[:, hh * MLA_PAD:(hh + 1) * MLA_PAD] for hh in range(2)]
    ndiag = tq // tk

    def step(kb, carry, diag):
        m, l, acc = carry
        start = pl.multiple_of(kb * tk, tk)
        s = jnp.concatenate([_dot_nt(k_ref[pl.ds(start, tk), hh * MLA_PAD:(hh + 1) * MLA_PAD], qs[hh])
                             for hh in range(2)], axis=-1)
        if diag is not None:
            s = jnp.where(krow + diag * tk <= qidx, s, NEG)
        m_new = jnp.maximum(m, jnp.max(s, axis=0, keepdims=True))
        alpha = jnp.exp2(m - m_new)
        p = jnp.exp2(s - m_new)
        l = l * alpha + jnp.sum(p, axis=0, keepdims=True)
        acc = acc * alpha + lax.dot_general(v_ref[pl.ds(start, tk), :], p.astype(BF16),
                                            (((0,), (0,)), ((), ())), preferred_element_type=F32)
        return m_new, l, acc

    carry = (jnp.full((1, 2 * tq), NEG, F32), jnp.zeros((1, 2 * tq), F32), jnp.zeros((LANES, 2 * tq), F32))
    carry = lax.fori_loop(0, i * ndiag, lambda kb, c: step(kb, c, None), carry)
    for j in range(ndiag):
        carry = step(i * ndiag + j, carry, j)
    _, l, acc = carry
    out = acc / l
    dim = lax.broadcasted_iota(jnp.int32, (LANES, tq), 0)
    o_ref[...] = jnp.where(dim < MLA_V, out[:, :tq], out[:, tq:]).T.astype(BF16)


def _mla(q, k, v, B, S, tq, tk):
    nq = S // tq
    npair = MLA_HEADS // 2
    return pl.pallas_call(
        functools.partial(_mla_kernel, tq=tq, tk=tk),
        grid=(B, npair, nq),
        in_specs=[pl.BlockSpec((tq, 2 * MLA_PAD), lambda b, j, i: (b * nq + i, j)),
                  pl.BlockSpec((S, 2 * MLA_PAD), lambda b, j, i: (b, j)),
                  pl.BlockSpec((S, LANES), lambda b, j, i: (b, j))],
        out_specs=pl.BlockSpec((tq, LANES), lambda b, j, i: (b * nq + i, j)),
        out_shape=jax.ShapeDtypeStruct((B * S, MLA_OUT), BF16),
        compiler_params=pltpu.CompilerParams(vmem_limit_bytes=VMEM_LIMIT),
        name="mla",
    )(q, k, v)


def _sb_kernel(q_ref, k_ref, v_ref, o_ref, *, tq):
    i = pl.program_id(2)
    r = lax.broadcasted_iota(jnp.int32, (tq, tq), 0)
    c = lax.broadcasted_iota(jnp.int32, (tq, tq), 1)
    later = jnp.where(c > r, 1.0, 0.0).astype(BF16)
    lane = lax.broadcasted_iota(jnp.int32, (tq, LANES), 1)
    q = q_ref[...]
    q2 = jnp.concatenate([jnp.where(lane < HEAD_DIM, q, jnp.zeros_like(q)),
                          jnp.where(lane >= HEAD_DIM, q, jnp.zeros_like(q))], axis=0)

    def log_terms(start, nkeys):
        z = _dot_nt(k_ref[pl.ds(start, nkeys), :], q2)
        log_beta = jnp.minimum(z, 0.0) - jnp.log(1.0 + jnp.exp2(-jnp.abs(z))) * LOG2E
        return log_beta, log_beta - z

    def later_sum(lom):
        hi = lom.astype(BF16)
        lo = (lom - hi.astype(F32)).astype(BF16)
        return _dot(later, hi) + _dot(later, lo)

    def weighted_values(start, nkeys, w):
        return lax.dot_general(v_ref[pl.ds(start, nkeys), :], w.astype(BF16), (((0,), (0,)), ((), ())),
                               preferred_element_type=F32)

    def step(kb, carry):
        after, acc = carry
        start = pl.multiple_of(kb * tq, tq)
        log_beta, lom = log_terms(start, tq)
        w = jnp.exp2(log_beta + later_sum(lom) + after)
        return after + jnp.sum(lom, axis=0, keepdims=True), acc + weighted_values(start, tq, w)

    def first_step():
        start = pl.multiple_of(jnp.maximum(i - 1, 0) * tq, tq)
        kpos = start + lax.broadcasted_iota(jnp.int32, (2 * tq, 2 * tq), 0)
        qcol = lax.broadcasted_iota(jnp.int32, (2 * tq, 2 * tq), 1)
        strict = kpos < i * tq + jnp.where(qcol >= tq, qcol - tq, qcol)
        log_beta, lom = log_terms(start, 2 * tq)
        lom = jnp.where(strict, lom, 0.0)
        sum_hi = jnp.sum(lom[tq:], axis=0, keepdims=True)
        inner = jnp.concatenate([later_sum(lom[:tq]) + sum_hi, later_sum(lom[tq:])], axis=0)
        w = jnp.where(strict, jnp.exp2(log_beta + inner), 0.0)
        return sum_hi + jnp.sum(lom[:tq], axis=0, keepdims=True), weighted_values(start, 2 * tq, w)

    def cond(state):
        n, top, _ = state
        return jnp.logical_and(n < i - 1, top > SB_DEAD_LOG2)

    def body(state):
        n, _, carry = state
        carry = step(i - 2 - n, carry)
        return n + 1, jnp.max(carry[0]), carry

    carry = first_step()
    _, _, (_, acc) = lax.while_loop(cond, body, (jnp.int32(0), jnp.max(carry[0]), carry))
    dim = lax.broadcasted_iota(jnp.int32, (LANES, tq), 0)
    o_ref[...] = jnp.where(dim < HEAD_DIM, acc[:, :tq], acc[:, tq:]).T.astype(BF16)


def _sb(q, k, v, B, S, tq):
    nq = S // tq
    w = LANES
    return pl.pallas_call(
        functools.partial(_sb_kernel, tq=tq),
        grid=(B, SB_W // w, nq),
        in_specs=[pl.BlockSpec((tq, w), lambda b, j, i: (b * nq + i, j)),
                  pl.BlockSpec((S, w), lambda b, j, i: (b, j)),
                  pl.BlockSpec((S, w), lambda b, j, i: (b, j))],
        out_specs=pl.BlockSpec((tq, w), lambda b, j, i: (b * nq + i, j)),
        out_shape=jax.ShapeDtypeStruct((B * S, SB_W), BF16),
        compiler_params=pltpu.CompilerParams(vmem_limit_bytes=VMEM_LIMIT),
        name="sb",
    )(q, k, v)


def _merge_kernel(x_ref, gn_ref, *rest):
    ng = len(DSA_GROUPS)
    oa_refs, lse_refs = rest[:ng], rest[ng:2 * ng]
    ob_ref, oc_ref, wg_ref, wb_ref, wo_ref, out_ref = rest[2 * ng:2 * ng + 6]
    scr = iter(rest[2 * ng + 6:])
    x = x_ref[...]
    tm = x.shape[0]
    h = _rms(x, gn_ref[...]).astype(BF16)

    def token_major(ref, d):
        if d == 1:
            return ref[0]
        buf = next(scr)
        nh = DSA_OUT // LANES
        for r in range(d):
            for half in range(nh):
                buf[half, pl.ds(r, tm // d, stride=d), :] = ref[r, :, half * LANES:(half + 1) * LANES]
        return jnp.concatenate([buf[half] for half in range(nh)], axis=-1)

    def branch_dots(b, o):
        r0 = (0, DSA_OUT, DSA_OUT + MLA_OUT)[b]
        return _dot(h, wg_ref[:, b * D_MODEL:(b + 1) * D_MODEL]), _dot(o, wb_ref[r0:r0 + o.shape[1], :])

    def gated(dots):
        return jax.nn.sigmoid(dots[0]) * dots[1]

    dots_b = branch_dots(1, ob_ref[...])
    os_ = [token_major(oa_refs[g], d) for g, (_, d) in enumerate(DSA_GROUPS)]
    l0, l1, l2 = [token_major(lse_refs[g], d) for g, (_, d) in enumerate(DSA_GROUPS)]
    m = jnp.maximum(jnp.maximum(l0, l1), l2)
    e0, e1, e2 = jnp.exp(l0 - m), jnp.exp(l1 - m), jnp.exp(l2 - m)
    oa = ((e0 * os_[0] + e1 * os_[1] + e2 * os_[2]) / (e0 + e1 + e2)).astype(BF16)
    dots_c = branch_dots(2, oc_ref[...])
    merged = gated(dots_b)
    dots_a = branch_dots(0, oa)
    merged = merged + gated(dots_c)
    merged = merged + gated(dots_a)
    out_ref[...] = x + _dot(merged.astype(BF16), wo_ref[...])


def _merge(x, p, oas, lses, ob, oc, S, tm):
    T = x.shape[0]
    ns = S // tm
    row = lambda w: pl.BlockSpec((tm, w), lambda i: (i, 0))
    grp = [pl.BlockSpec((None, d, tm // d, DSA_OUT), lambda i: (i // ns, 0, i % ns, 0)) for _, d in DSA_GROUPS]
    consts = [p['wg'], p['wb'], p['wo']]
    return pl.pallas_call(
        _merge_kernel,
        grid=(T // tm,),
        in_specs=[row(D_MODEL), _const_spec(p['gn'].shape)] + grp + grp + [row(MLA_OUT), row(SB_W)]
                 + [_const_spec(c.shape) for c in consts],
        out_specs=row(D_MODEL),
        out_shape=jax.ShapeDtypeStruct((T, D_MODEL), F32),
        scratch_shapes=[pltpu.VMEM((DSA_OUT // LANES, tm, LANES), F32)
                        for _ in range(2) for _, d in DSA_GROUPS if d > 1],
        compiler_params=pltpu.CompilerParams(vmem_limit_bytes=VMEM_LIMIT),
        name="merge",
    )(x, p['gn'], *oas, *lses, ob, oc, *consts)


def _mlp_kernel(x_ref, g_ref, w1_ref, w2_ref, out_ref, *, chunk):
    x = x_ref[...]
    h = _rms(x, g_ref[...]).astype(BF16)
    acc = x
    for c in range(D_FF // chunk):
        a = jnp.maximum(_dot(h, w1_ref[:, c * chunk:(c + 1) * chunk]), 0.0)
        acc = acc + _dot((a * a).astype(BF16), w2_ref[c * chunk:(c + 1) * chunk, :])
    out_ref[...] = acc


def _mlp(x, g, w1, w2, tm, chunk=512):
    T = x.shape[0]
    row = pl.BlockSpec((tm, D_MODEL), lambda i: (i, 0))
    return pl.pallas_call(
        functools.partial(_mlp_kernel, chunk=chunk),
        grid=(T // tm,),
        in_specs=[row, _const_spec(g.shape), _const_spec(w1.shape), _const_spec(w2.shape)],
        out_specs=row,
        out_shape=jax.ShapeDtypeStruct((T, D_MODEL), F32),
        compiler_params=pltpu.CompilerParams(vmem_limit_bytes=VMEM_LIMIT),
        name="mlp",
    )(x, g, w1, w2)


def _rope_lane_tables(S, dim, first_lane, period):
    half = dim // 2
    pos = jnp.arange(S, dtype=F32)
    inv_freq = ROPE_THETA ** (-jnp.arange(0, dim, 2, dtype=F32) / dim)
    ang = pos[:, None] * inv_freq[None, :]
    cos, sin = jnp.cos(ang), jnp.sin(ang)
    rel = jnp.arange(LANES) % period - first_lane
    idx = jnp.clip(rel, 0, dim - 1) % half
    in1 = (rel >= 0) & (rel < half)
    in2 = (rel >= half) & (rel < dim)
    c = jnp.where((in1 | in2)[None, :], cos[:, idx], 1.0)
    s1 = jnp.where(in2[None, :], sin[:, idx], 0.0)
    s2 = jnp.where(in1[None, :], -sin[:, idx], 0.0)
    return c, s1, s2


def _block_diag_ones(block):
    r = jnp.arange(MXU_DIM) // block
    return (r[:, None] == r[None, :]).astype(BF16)


def _prep_params(attn_norm, w_in, a_q_norm, a_k_norm, b_q_a_norm, w_q_b, b_kv_a_norm, w_kv_b,
                 b_q_norm, b_k_norm, w_branch, w_out, mlp_norm, w_ff1, w_ff2):
    L = w_in.shape[0]
    o_ql = 3 * DSA_W
    o_kr = o_ql + MLA_Q_LORA + MLA_KV_LORA
    o_c = o_kr + MLA_ROPE
    o_g = o_c + 3 * SB_W
    zeros = lambda n: jnp.zeros((L, D_MODEL, n), F32)
    ws = jnp.concatenate([w_in[:, :, o_ql:o_kr], zeros(MLA_NOPE), w_in[:, :, o_kr:o_c],
                          zeros(MLA_PAD - MLA_QK)], axis=-1)
    pad_head = lambda w: jnp.pad(w, ((0, 0), (0, 0), (0, 0), (0, MLA_PAD - w.shape[-1]))).reshape(
        L, w.shape[1], MLA_HEADS * MLA_PAD)
    pad_gain = lambda g: jnp.tile(jnp.pad(g, ((0, 0), (0, MLA_PAD - MLA_QK))), (1, MLA_HEADS))[:, None, :]
    n_heads_a = DSA_W // HEAD_DIM
    return {
        'gn': attn_norm[:, None, :],
        'wa': w_in[:, :, :o_ql].astype(BF16),
        'ws': ws.astype(BF16),
        'wc': w_in[:, :, o_c:o_g].astype(BF16),
        'wg': w_in[:, :, o_g:].astype(BF16),
        'wqb': pad_head(w_q_b).astype(BF16),
        'wk': pad_head(w_kv_b[..., :MLA_NOPE]).astype(BF16),
        'wv': w_kv_b[..., MLA_NOPE:].reshape(L, MLA_KV_LORA, MLA_OUT).astype(BF16),
        'gaq': jnp.tile(a_q_norm * HEAD_DIM ** -0.5, (1, n_heads_a))[:, None, :],
        'gak': jnp.tile(a_k_norm, (1, n_heads_a))[:, None, :],
        'gql': b_q_a_norm[:, None, :],
        'gkvl': b_kv_a_norm[:, None, :],
        'gbq': pad_gain(b_q_norm * (MLA_QK ** -0.5 * LOG2E)),
        'gbk': pad_gain(b_k_norm),
        'wb': w_branch.astype(BF16),
        'wo': w_out.astype(BF16),
        'gm': mlp_norm[:, None, :],
        'w1': w_ff1.astype(BF16),
        'w2': w_ff2.astype(BF16),
    }


def kernel(x, attn_norm, w_in, a_q_norm, a_k_norm, b_q_a_norm, w_q_b, b_kv_a_norm, w_kv_b, b_q_norm,
           b_k_norm, w_branch, w_out, mlp_norm, w_ff1, w_ff2):
    B, S, _ = x.shape
    depth = w_in.shape[0]
    params = _prep_params(attn_norm, w_in, a_q_norm, a_k_norm, b_q_a_norm, w_q_b, b_kv_a_norm, w_kv_b,
                          b_q_norm, b_k_norm, w_branch, w_out, mlp_norm, w_ff1, w_ff2)
    tabs = {
        'rope': _rope_lane_tables(S, ROT_DIM, 0, HEAD_DIM) + _rope_lane_tables(S, MLA_ROPE, MLA_NOPE, MLA_PAD),
        'm64': _block_diag_ones(HEAD_DIM),
        'm128': _block_diag_ones(MLA_PAD),
    }
    tm = min(512, S)
    xt = x.reshape(B * S, D_MODEL)
    for l in range(depth):
        p = {k: v[l] for k, v in params.items()}
        a_qkv, (bq, bk, bv, cq, ck, cv) = _proj(xt, p, tabs, S, tm)

        oas, lses = [], []
        ng = len(DSA_GROUPS)
        for g, (window, d) in enumerate(DSA_GROUPS):
            assert window // d == BLOCK and S % (BLOCK * d) == 0 and (B * S) % (DSA_BLOCKS_PER_STEP * BLOCK) == 0
            qg, kg, vg = (a_qkv[t * ng + g].reshape(B * S, DSA_OUT) for t in range(3))
            o, lse = _dsa(qg, kg, vg, S // d // BLOCK)
            oas.append(o.reshape(B, d, S // d, DSA_OUT))
            lses.append(lse.reshape(B, d, S // d, DSA_OUT))
        ob = _mla(bq, bk, bv, B, S, min(MLA_TQ, S), min(MLA_TK, S))
        oc = _sb(cq, ck, cv, B, S, min(SB_TQ, S))

        xt = _merge(xt, p, oas, lses, ob, oc, S, min(256, S))
        xt = _mlp(xt, p['gm'], p['w1'], p['w2'], tm)
    return xt.reshape(B, S, D_MODEL)
```

```python
import functools

import jax
import jax.numpy as jnp
from jax import lax
from jax.experimental import pallas as pl
from jax.experimental.pallas import tpu as pltpu

D_MODEL = 1024
HEAD_DIM = 64
ROPE_THETA = 500000.0
ROT_DIM = HEAD_DIM // 4
NORM_EPS = 1e-6
BLOCK = 128

DSA_GROUPS = ((128, 1), (512, 4), (2048, 16))
DSA_HEADS_PER_GROUP = 4
DSA_W = DSA_HEADS_PER_GROUP * len(DSA_GROUPS) * HEAD_DIM
DSA_OUT = DSA_HEADS_PER_GROUP * HEAD_DIM

MLA_HEADS = 8
MLA_Q_LORA = 256
MLA_KV_LORA = 128
MLA_NOPE = 64
MLA_ROPE = 32
MLA_QK = MLA_NOPE + MLA_ROPE
MLA_V = 64
MLA_OUT = MLA_HEADS * MLA_V
MLA_PAD = 128

SB_HEADS = 8
SB_W = SB_HEADS * HEAD_DIM

N_BRANCH = 3
D_FF = 4 * D_MODEL

LANES = 128
MXU_DIM = 256
VMEM_LIMIT = 56 * 1024 * 1024
MLA_TQ, MLA_TK = 512, 512
SB_TQ = 256
SB_PAIRS_PER_STEP = 2
DSA_BLOCKS_PER_STEP = 4

F32 = jnp.float32
BF16 = jnp.bfloat16
NEG = -1e30
LOG2E = 1.4426950408889634
SB_DEAD_LOG2 = -160.0


def _dot(a, b):
    return jnp.dot(a, b, preferred_element_type=F32)


def _dot_nt(a, b):
    return lax.dot_general(a, b, (((1,), (1,)), ((), ())), preferred_element_type=F32)


def _split_dot(s, m):
    hi = s.astype(BF16)
    lo = (s - hi.astype(F32)).astype(BF16)
    return _dot(hi, m) + _dot(lo, m)


def _rms(x, g):
    ms = jnp.mean(x * x, axis=-1, keepdims=True)
    return x * lax.rsqrt(ms + NORM_EPS) * g


def _head_norm(y, m, inv_n, g):
    ss = _split_dot(y * y, m)
    return y * lax.rsqrt(ss * inv_n + NORM_EPS) * g


def _rope(y, c, s1, s2, shift):
    outs = []
    for j in range(y.shape[1] // LANES):
        yj = y[:, j * LANES:(j + 1) * LANES]
        outs.append(yj * c + pltpu.roll(yj, shift, 1) * s1 + pltpu.roll(yj, LANES - shift, 1) * s2)
    return jnp.concatenate(outs, axis=-1)


def _proj_kernel(x_ref, gn_ref, wa_ref, ws_ref, wc_ref, wqb_ref, wk_ref, wv_ref,
                 gaq_ref, gak_ref, gql_ref, gkvl_ref, gbq_ref, gbk_ref,
                 ca_ref, s1a_ref, s2a_ref, cm_ref, s1m_ref, s2m_ref, m64_ref, m128_ref,
                 *rest):
    n_a = 3 * len(DSA_GROUPS)
    a_refs, (bq_ref, bk_ref, bv_ref, cq_ref, ck_ref, cv_ref) = rest[:n_a], rest[n_a:n_a + 6]
    scr_refs = rest[n_a + 6:]
    h = _rms(x_ref[...], gn_ref[...]).astype(BF16)
    ca, s1a, s2a = ca_ref[...], s1a_ref[...], s2a_ref[...]
    cm, s1m, s2m = cm_ref[...], s1m_ref[...], s2m_ref[...]
    m64, m128 = m64_ref[...], m128_ref[...]
    W = MXU_DIM
    tm = x_ref.shape[0]

    jobs = []
    low = {}

    def low_rank(ys):
        low['ql'] = _rms(ys[:, :MLA_Q_LORA], gql_ref[...]).astype(BF16)
        low['kvl'] = _rms(ys[:, MLA_Q_LORA:MLA_Q_LORA + MLA_KV_LORA], gkvl_ref[...]).astype(BF16)
        kr = ys[:, MLA_Q_LORA + MLA_KV_LORA:]
        low['kr2'] = jnp.concatenate([kr, kr], axis=-1)
    jobs.append((lambda: _dot(h, ws_ref[...]), low_rank))

    scr = iter(scr_refs)
    for t, g_ref in enumerate((gaq_ref, gak_ref, None)):
        for g, (_, d) in enumerate(DSA_GROUPS):
            def a_epilogue(y, g_ref=g_ref, g=g, d=d, dst=a_refs[t * len(DSA_GROUPS) + g],
                           buf=next(scr) if d > 1 else None):
                if g_ref is not None:
                    y = _head_norm(y, m64, 1.0 / HEAD_DIM, g_ref[:, g * W:(g + 1) * W])
                    y = _rope(y, ca, s1a, s2a, ROT_DIM // 2)
                if d == 1:
                    dst[0] = y.astype(BF16)
                    return
                for half in range(W // LANES):
                    buf[half] = y[:, half * LANES:(half + 1) * LANES]
                for r in range(d):
                    for half in range(W // LANES):
                        dst[r, :, half * LANES:(half + 1) * LANES] = (
                            buf[half, pl.ds(r, tm // d, stride=d), :].astype(BF16))
            col = t * DSA_W + g * W
            jobs.append((lambda col=col: _dot(h, wa_ref[:, col:col + W]), a_epilogue))

    for n, (dst, scale) in enumerate(((cq_ref, HEAD_DIM ** -0.5 * LOG2E), (ck_ref, None), (cv_ref, None))):
        def c_epilogue(y, dst=dst, scale=scale):
            dst[...] = (y if scale is None else y * scale).astype(BF16)
        jobs.append((lambda n=n: _dot(h, wc_ref[:, n * SB_W:(n + 1) * SB_W]), c_epilogue))

    for c in range(MLA_HEADS * MLA_PAD // W):
        cols = slice(c * W, (c + 1) * W)

        def q_epilogue(y, cols=cols):
            y = _head_norm(y, m128, 1.0 / MLA_QK, gbq_ref[:, cols])
            bq_ref[:, cols] = _rope(y, cm, s1m, s2m, MLA_ROPE // 2).astype(BF16)

        def k_epilogue(y, cols=cols):
            y = _head_norm(y + low['kr2'], m128, 1.0 / MLA_QK, gbk_ref[:, cols])
            bk_ref[:, cols] = _rope(y, cm, s1m, s2m, MLA_ROPE // 2).astype(BF16)

        jobs.append((lambda cols=cols: _dot(low['ql'], wqb_ref[:, cols]), q_epilogue))
        jobs.append((lambda cols=cols: _dot(low['kvl'], wk_ref[:, cols]), k_epilogue))

    def v_epilogue(y):
        bv_ref[...] = y.astype(BF16)
    jobs.append((lambda: _dot(low['kvl'], wv_ref[...]), v_epilogue))

    y = jobs[0][0]()
    for j, (_, epilogue) in enumerate(jobs):
        y_next = jobs[j + 1][0]() if j + 1 < len(jobs) else None
        epilogue(y)
        y = y_next


def _const_spec(shape):
    nd = len(shape)
    return pl.BlockSpec(shape, lambda *_: (0,) * nd, pipeline_mode=pl.Buffered(1))


def _proj(x, p, tabs, S, tm):
    T = x.shape[0]
    ns = S // tm
    row = lambda w: pl.BlockSpec((tm, w), lambda i: (i, 0))
    tab = pl.BlockSpec((tm, LANES), lambda i: (i % ns, 0))
    consts = [p['gn'], p['wa'], p['ws'], p['wc'], p['wqb'], p['wk'], p['wv'],
              p['gaq'], p['gak'], p['gql'], p['gkvl'], p['gbq'], p['gbk']]
    ins = [x] + consts + list(tabs['rope']) + [tabs['m64'], tabs['m128']]
    in_specs = ([row(D_MODEL)] + [_const_spec(c.shape) for c in consts] + [tab] * 6
                + [_const_spec((MXU_DIM, MXU_DIM))] * 2)
    widths = (MLA_HEADS * MLA_PAD, MLA_HEADS * MLA_PAD, MLA_OUT, SB_W, SB_W, SB_W)
    B = T // S
    dils = [d for _ in range(3) for _, d in DSA_GROUPS]
    a_specs = [pl.BlockSpec((None, d, tm // d, DSA_OUT), lambda i: (i // ns, 0, i % ns, 0)) for d in dils]
    a_shapes = [jax.ShapeDtypeStruct((B, d, S // d, DSA_OUT), BF16) for d in dils]
    outs = pl.pallas_call(
        _proj_kernel,
        grid=(T // tm,),
        in_specs=in_specs,
        out_specs=a_specs + [row(w) for w in widths],
        out_shape=a_shapes + [jax.ShapeDtypeStruct((T, w), BF16) for w in widths],
        scratch_shapes=[pltpu.VMEM((DSA_OUT // LANES, tm, LANES), F32) for d in dils if d > 1],
        compiler_params=pltpu.CompilerParams(vmem_limit_bytes=VMEM_LIMIT),
        name="proj",
    )(*ins)
    return outs[:len(dils)], outs[len(dils):]


def _dsa_kernel(q_ref, kp_ref, kc_ref, vp_ref, vc_ref, o_ref, lse_ref, *, nb_seq):
    n = pl.program_id(0)
    nblk = DSA_BLOCKS_PER_STEP
    pens = [jnp.where((nblk * n + blk) % nb_seq == 0, NEG, 0.0).astype(F32) for blk in range(nblk)]
    qi = lax.broadcasted_iota(jnp.int32, (BLOCK, 2 * BLOCK), 0)
    kj = lax.broadcasted_iota(jnp.int32, (BLOCK, 2 * BLOCK), 1)
    band = (kj >= qi) & (kj <= qi + BLOCK)
    lane = lax.broadcasted_iota(jnp.int32, (BLOCK, LANES), 1)
    for blk in range(nblk):
        rows = slice(blk * BLOCK, (blk + 1) * BLOCK)
        keys = slice((blk - 1) * BLOCK, (blk + 1) * BLOCK)
        pen_row = jnp.where(kj < BLOCK, pens[blk], 0.0)
        for pr in range(DSA_OUT // LANES):
            sl = slice(pr * LANES, (pr + 1) * LANES)
            q = q_ref[rows, sl]
            if blk == 0:
                k2 = jnp.concatenate([kp_ref[:, sl], kc_ref[:BLOCK, sl]], axis=0)
                v2 = jnp.concatenate([vp_ref[:, sl], vc_ref[:BLOCK, sl]], axis=0)
            else:
                k2, v2 = kc_ref[keys, sl], vc_ref[keys, sl]
            o_acc = l_acc = None
            for hh in range(2):
                sel = (lane < HEAD_DIM) if hh == 0 else (lane >= HEAD_DIM)
                qh = jnp.where(sel, q, jnp.zeros_like(q))
                s = jnp.where(band, _dot_nt(qh, k2) + pen_row, NEG)
                m = jnp.max(s, axis=-1, keepdims=True)
                p = jnp.exp(s - m)
                den = jnp.sum(p, axis=-1, keepdims=True)
                o = _dot(p.astype(BF16), v2) / den
                lse = m + jnp.log(den)
                o_acc = o if hh == 0 else jnp.where(sel, o, o_acc)
                l_acc = jnp.broadcast_to(lse, o.shape) if hh == 0 else jnp.where(sel, lse, l_acc)
            o_ref[rows, sl] = o_acc
            lse_ref[rows, sl] = l_acc


def _dsa(q, k, v, nb_seq):
    T = q.shape[0]
    nblk = DSA_BLOCKS_PER_STEP
    cur = pl.BlockSpec((nblk * BLOCK, DSA_OUT), lambda n: (n, 0))
    prev = pl.BlockSpec((BLOCK, DSA_OUT), lambda n: (jnp.maximum(nblk * n - 1, 0), 0))
    return pl.pallas_call(
        functools.partial(_dsa_kernel, nb_seq=nb_seq),
        grid=(T // (nblk * BLOCK),),
        in_specs=[cur, prev, cur, prev, cur],
        out_specs=[cur, cur],
        out_shape=[jax.ShapeDtypeStruct((T, DSA_OUT), F32)] * 2,
        name="dsa",
    )(q, k, k, v, v)


def _mla_kernel(q_ref, k_ref, v_ref, o_ref, *, tq, tk):
    i = pl.program_id(2)
    krow = lax.broadcasted_iota(jnp.int32, (tk, 2 * tq), 0)
    qcol = lax.broadcasted_iota(jnp.int32, (tk, 2 * tq), 1)
    qidx = jnp.where(qcol >= tq, qcol - tq, qcol)
    qs = [q_ref[:, hh * MLA_PAD:(hh + 1) * MLA_PAD] for hh in range(2)]
    ndiag = tq // tk

    def scores(kb):
        start = pl.multiple_of(kb * tk, tk)
        return jnp.concatenate([_dot_nt(k_ref[pl.ds(start, tk), hh * MLA_PAD:(hh + 1) * MLA_PAD], qs[hh])
                                for hh in range(2)], axis=-1)

    def absorb(kb, s, carry, diag):
        m, l, acc = carry
        start = pl.multiple_of(kb * tk, tk)
        if diag is not None:
            s = jnp.where(krow + diag * tk <= qidx, s, NEG)
        m_new = jnp.maximum(m, jnp.max(s, axis=0, keepdims=True))
        alpha = jnp.exp2(m - m_new)
        p = jnp.exp2(s - m_new)
        l = l * alpha + jnp.sum(p, axis=0, keepdims=True)
        acc = acc * alpha + lax.dot_general(v_ref[pl.ds(start, tk), :], p.astype(BF16),
                                            (((0,), (0,)), ((), ())), preferred_element_type=F32)
        return m_new, l, acc

    def absorb_all(tiles, carry):
        s = scores(tiles[0][0])
        for t, (kb, diag) in enumerate(tiles):
            s_next = scores(tiles[t + 1][0]) if t + 1 < len(tiles) else None
            carry = absorb(kb, s, carry, diag)
            s = s_next
        return carry

    nfull = i * ndiag
    carry = (jnp.full((1, 2 * tq), NEG, F32), jnp.zeros((1, 2 * tq), F32), jnp.zeros((LANES, 2 * tq), F32))
    carry = lax.fori_loop(0, nfull // 2,
                          lambda n, c: absorb_all([(2 * n, None), (2 * n + 1, None)], c), carry)
    diag_tiles = [(nfull + j, j) for j in range(ndiag)]
    carry = lax.cond(nfull % 2 == 1,
                     lambda c: absorb_all([(nfull - 1, None)] + diag_tiles, c),
                     lambda c: absorb_all(diag_tiles, c), carry)
    _, l, acc = carry
    out = acc / l
    dim = lax.broadcasted_iota(jnp.int32, (LANES, tq), 0)
    o_ref[...] = jnp.where(dim < MLA_V, out[:, :tq], out[:, tq:]).T.astype(BF16)


def _mla(q, k, v, B, S, tq, tk):
    nq = S // tq
    npair = MLA_HEADS // 2
    return pl.pallas_call(
        functools.partial(_mla_kernel, tq=tq, tk=tk),
        grid=(B, npair, nq),
        in_specs=[pl.BlockSpec((tq, 2 * MLA_PAD), lambda b, j, i: (b * nq + i, j)),
                  pl.BlockSpec((S, 2 * MLA_PAD), lambda b, j, i: (b, j)),
                  pl.BlockSpec((S, LANES), lambda b, j, i: (b, j))],
        out_specs=pl.BlockSpec((tq, LANES), lambda b, j, i: (b * nq + i, j)),
        out_shape=jax.ShapeDtypeStruct((B * S, MLA_OUT), BF16),
        compiler_params=pltpu.CompilerParams(vmem_limit_bytes=VMEM_LIMIT),
        name="mla",
    )(q, k, v)


def _sb_kernel(q_ref, k_ref, v_ref, o_ref, *, tq):
    for pr in range(q_ref.shape[1] // LANES):
        _sb_pair(q_ref, k_ref, v_ref, o_ref, slice(pr * LANES, (pr + 1) * LANES), pl.program_id(2), tq)


def _sb_pair(q_ref, k_ref, v_ref, o_ref, cs, i, tq):
    r = lax.broadcasted_iota(jnp.int32, (tq, tq), 0)
    c = lax.broadcasted_iota(jnp.int32, (tq, tq), 1)
    later = jnp.where(c > r, 1.0, 0.0).astype(BF16)
    lane = lax.broadcasted_iota(jnp.int32, (tq, LANES), 1)
    q = q_ref[:, cs]
    q2 = jnp.concatenate([jnp.where(lane < HEAD_DIM, q, jnp.zeros_like(q)),
                          jnp.where(lane >= HEAD_DIM, q, jnp.zeros_like(q))], axis=0)

    def log_terms(start, nkeys):
        z = _dot_nt(k_ref[pl.ds(start, nkeys), cs], q2)
        log_beta = jnp.minimum(z, 0.0) - jnp.log(1.0 + jnp.exp2(-jnp.abs(z))) * LOG2E
        return log_beta, log_beta - z

    def later_sum(lom):
        hi = lom.astype(BF16)
        lo = (lom - hi.astype(F32)).astype(BF16)
        return _dot(later, hi) + _dot(later, lo)

    def weighted_values(start, nkeys, w):
        return lax.dot_general(v_ref[pl.ds(start, nkeys), cs], w.astype(BF16), (((0,), (0,)), ((), ())),
                               preferred_element_type=F32)

    def step(kb, carry):
        after, acc = carry
        start = pl.multiple_of(kb * tq, tq)
        log_beta, lom = log_terms(start, tq)
        w = jnp.exp2(log_beta + later_sum(lom) + after)
        return after + jnp.sum(lom, axis=0, keepdims=True), acc + weighted_values(start, tq, w)

    def first_step():
        start = pl.multiple_of(jnp.maximum(i - 1, 0) * tq, tq)
        kpos = start + lax.broadcasted_iota(jnp.int32, (2 * tq, 2 * tq), 0)
        qcol = lax.broadcasted_iota(jnp.int32, (2 * tq, 2 * tq), 1)
        strict = kpos < i * tq + jnp.where(qcol >= tq, qcol - tq, qcol)
        log_beta, lom = log_terms(start, 2 * tq)
        lom = jnp.where(strict, lom, 0.0)
        sum_hi = jnp.sum(lom[tq:], axis=0, keepdims=True)
        inner = jnp.concatenate([later_sum(lom[:tq]) + sum_hi, later_sum(lom[tq:])], axis=0)
        w = jnp.where(strict, jnp.exp2(log_beta + inner), 0.0)
        return sum_hi + jnp.sum(lom[:tq], axis=0, keepdims=True), weighted_values(start, 2 * tq, w)

    def cond(state):
        n, top, _ = state
        return jnp.logical_and(n < i - 1, top > SB_DEAD_LOG2)

    def body(state):
        n, _, carry = state
        carry = step(i - 2 - n, carry)
        return n + 1, jnp.max(carry[0]), carry

    carry = first_step()
    _, _, (_, acc) = lax.while_loop(cond, body, (jnp.int32(0), jnp.max(carry[0]), carry))
    dim = lax.broadcasted_iota(jnp.int32, (LANES, tq), 0)
    o_ref[:, cs] = jnp.where(dim < HEAD_DIM, acc[:, :tq], acc[:, tq:]).T.astype(BF16)


def _sb(q, k, v, B, S, tq):
    nq = S // tq
    w = SB_PAIRS_PER_STEP * LANES
    return pl.pallas_call(
        functools.partial(_sb_kernel, tq=tq),
        grid=(B, SB_W // w, nq),
        in_specs=[pl.BlockSpec((tq, w), lambda b, j, i: (b * nq + i, j)),
                  pl.BlockSpec((S, w), lambda b, j, i: (b, j)),
                  pl.BlockSpec((S, w), lambda b, j, i: (b, j))],
        out_specs=pl.BlockSpec((tq, w), lambda b, j, i: (b * nq + i, j)),
        out_shape=jax.ShapeDtypeStruct((B * S, SB_W), BF16),
        compiler_params=pltpu.CompilerParams(vmem_limit_bytes=VMEM_LIMIT),
        name="sb",
    )(q, k, v)


def _merge_kernel(x_ref, gn_ref, *rest):
    ng = len(DSA_GROUPS)
    oa_refs, lse_refs = rest[:ng], rest[ng:2 * ng]
    ob_ref, oc_ref, wg_ref, wb_ref, wo_ref, out_ref = rest[2 * ng:2 * ng + 6]
    scr = iter(rest[2 * ng + 6:])
    x = x_ref[...]
    tm = x.shape[0]
    h = _rms(x, gn_ref[...]).astype(BF16)

    def token_major(ref, d):
        if d == 1:
            return ref[0]
        buf = next(scr)
        nh = DSA_OUT // LANES
        for r in range(d):
            for half in range(nh):
                buf[half, pl.ds(r, tm // d, stride=d), :] = ref[r, :, half * LANES:(half + 1) * LANES]
        return jnp.concatenate([buf[half] for half in range(nh)], axis=-1)

    def branch_dots(b, o):
        r0 = (0, DSA_OUT, DSA_OUT + MLA_OUT)[b]
        return _dot(h, wg_ref[:, b * D_MODEL:(b + 1) * D_MODEL]), _dot(o, wb_ref[r0:r0 + o.shape[1], :])

    def gated(dots):
        return jax.nn.sigmoid(dots[0]) * dots[1]

    dots_b = branch_dots(1, ob_ref[...])
    os_ = [token_major(oa_refs[g], d) for g, (_, d) in enumerate(DSA_GROUPS)]
    l0, l1, l2 = [token_major(lse_refs[g], d) for g, (_, d) in enumerate(DSA_GROUPS)]
    m = jnp.maximum(jnp.maximum(l0, l1), l2)
    e0, e1, e2 = jnp.exp(l0 - m), jnp.exp(l1 - m), jnp.exp(l2 - m)
    oa = ((e0 * os_[0] + e1 * os_[1] + e2 * os_[2]) / (e0 + e1 + e2)).astype(BF16)
    dots_c = branch_dots(2, oc_ref[...])
    merged = gated(dots_b)
    dots_a = branch_dots(0, oa)
    merged = merged + gated(dots_c)
    merged = merged + gated(dots_a)
    out_ref[...] = x + _dot(merged.astype(BF16), wo_ref[...])


def _merge(x, p, oas, lses, ob, oc, S, tm):
    T = x.shape[0]
    ns = S // tm
    row = lambda w: pl.BlockSpec((tm, w), lambda i: (i, 0))
    grp = [pl.BlockSpec((None, d, tm // d, DSA_OUT), lambda i: (i // ns, 0, i % ns, 0)) for _, d in DSA_GROUPS]
    consts = [p['wg'], p['wb'], p['wo']]
    return pl.pallas_call(
        _merge_kernel,
        grid=(T // tm,),
        in_specs=[row(D_MODEL), _const_spec(p['gn'].shape)] + grp + grp + [row(MLA_OUT), row(SB_W)]
                 + [_const_spec(c.shape) for c in consts],
        out_specs=row(D_MODEL),
        out_shape=jax.ShapeDtypeStruct((T, D_MODEL), F32),
        scratch_shapes=[pltpu.VMEM((DSA_OUT // LANES, tm, LANES), F32)
                        for _ in range(2) for _, d in DSA_GROUPS if d > 1],
        compiler_params=pltpu.CompilerParams(vmem_limit_bytes=VMEM_LIMIT),
        name="merge",
    )(x, p['gn'], *oas, *lses, ob, oc, *consts)


def _mlp_kernel(x_ref, g_ref, w1_ref, w2_ref, out_ref, *, chunk):
    x = x_ref[...]
    h = _rms(x, g_ref[...]).astype(BF16)
    acc = x
    for c in range(D_FF // chunk):
        a = jnp.maximum(_dot(h, w1_ref[:, c * chunk:(c + 1) * chunk]), 0.0)
        acc = acc + _dot((a * a).astype(BF16), w2_ref[c * chunk:(c + 1) * chunk, :])
    out_ref[...] = acc


def _mlp(x, g, w1, w2, tm, chunk=512):
    T = x.shape[0]
    row = pl.BlockSpec((tm, D_MODEL), lambda i: (i, 0))
    return pl.pallas_call(
        functools.partial(_mlp_kernel, chunk=chunk),
        grid=(T // tm,),
        in_specs=[row, _const_spec(g.shape), _const_spec(w1.shape), _const_spec(w2.shape)],
        out_specs=row,
        out_shape=jax.ShapeDtypeStruct((T, D_MODEL), F32),
        compiler_params=pltpu.CompilerParams(vmem_limit_bytes=VMEM_LIMIT),
        name="mlp",
    )(x, g, w1, w2)


def _rope_lane_tables(S, dim, first_lane, period):
    half = dim // 2
    pos = jnp.arange(S, dtype=F32)
    inv_freq = ROPE_THETA ** (-jnp.arange(0, dim, 2, dtype=F32) / dim)
    ang = pos[:, None] * inv_freq[None, :]
    cos, sin = jnp.cos(ang), jnp.sin(ang)
    rel = jnp.arange(LANES) % period - first_lane
    idx = jnp.clip(rel, 0, dim - 1) % half
    in1 = (rel >= 0) & (rel < half)
    in2 = (rel >= half) & (rel < dim)
    c = jnp.where((in1 | in2)[None, :], cos[:, idx], 1.0)
    s1 = jnp.where(in2[None, :], sin[:, idx], 0.0)
    s2 = jnp.where(in1[None, :], -sin[:, idx], 0.0)
    return c, s1, s2


def _block_diag_ones(block):
    r = jnp.arange(MXU_DIM) // block
    return (r[:, None] == r[None, :]).astype(BF16)


def _prep_params(attn_norm, w_in, a_q_norm, a_k_norm, b_q_a_norm, w_q_b, b_kv_a_norm, w_kv_b,
                 b_q_norm, b_k_norm, w_branch, w_out, mlp_norm, w_ff1, w_ff2):
    L = w_in.shape[0]
    o_ql = 3 * DSA_W
    o_kr = o_ql + MLA_Q_LORA + MLA_KV_LORA
    o_c = o_kr + MLA_ROPE
    o_g = o_c + 3 * SB_W
    zeros = lambda n: jnp.zeros((L, D_MODEL, n), F32)
    ws = jnp.concatenate([w_in[:, :, o_ql:o_kr], zeros(MLA_NOPE), w_in[:, :, o_kr:o_c],
                          zeros(MLA_PAD - MLA_QK)], axis=-1)
    pad_head = lambda w: jnp.pad(w, ((0, 0), (0, 0), (0, 0), (0, MLA_PAD - w.shape[-1]))).reshape(
        L, w.shape[1], MLA_HEADS * MLA_PAD)
    pad_gain = lambda g: jnp.tile(jnp.pad(g, ((0, 0), (0, MLA_PAD - MLA_QK))), (1, MLA_HEADS))[:, None, :]
    n_heads_a = DSA_W // HEAD_DIM
    return {
        'gn': attn_norm[:, None, :],
        'wa': w_in[:, :, :o_ql].astype(BF16),
        'ws': ws.astype(BF16),
        'wc': w_in[:, :, o_c:o_g].astype(BF16),
        'wg': w_in[:, :, o_g:].astype(BF16),
        'wqb': pad_head(w_q_b).astype(BF16),
        'wk': pad_head(w_kv_b[..., :MLA_NOPE]).astype(BF16),
        'wv': w_kv_b[..., MLA_NOPE:].reshape(L, MLA_KV_LORA, MLA_OUT).astype(BF16),
        'gaq': jnp.tile(a_q_norm * HEAD_DIM ** -0.5, (1, n_heads_a))[:, None, :],
        'gak': jnp.tile(a_k_norm, (1, n_heads_a))[:, None, :],
        'gql': b_q_a_norm[:, None, :],
        'gkvl': b_kv_a_norm[:, None, :],
        'gbq': pad_gain(b_q_norm * (MLA_QK ** -0.5 * LOG2E)),
        'gbk': pad_gain(b_k_norm),
        'wb': w_branch.astype(BF16),
        'wo': w_out.astype(BF16),
        'gm': mlp_norm[:, None, :],
        'w1': w_ff1.astype(BF16),
        'w2': w_ff2.astype(BF16),
    }


def kernel(x, attn_norm, w_in, a_q_norm, a_k_norm, b_q_a_norm, w_q_b, b_kv_a_norm, w_kv_b, b_q_norm,
           b_k_norm, w_branch, w_out, mlp_norm, w_ff1, w_ff2):
    B, S, _ = x.shape
    depth = w_in.shape[0]
    params = _prep_params(attn_norm, w_in, a_q_norm, a_k_norm, b_q_a_norm, w_q_b, b_kv_a_norm, w_kv_b,
                          b_q_norm, b_k_norm, w_branch, w_out, mlp_norm, w_ff1, w_ff2)
    tabs = {
        'rope': _rope_lane_tables(S, ROT_DIM, 0, HEAD_DIM) + _rope_lane_tables(S, MLA_ROPE, MLA_NOPE, MLA_PAD),
        'm64': _block_diag_ones(HEAD_DIM),
        'm128': _block_diag_ones(MLA_PAD),
    }
    tm = min(512, S)
    xt = x.reshape(B * S, D_MODEL)
    for l in range(depth):
        p = {k: v[l] for k, v in params.items()}
        a_qkv, (bq, bk, bv, cq, ck, cv) = _proj(xt, p, tabs, S, tm)

        oas, lses = [], []
        ng = len(DSA_GROUPS)
        for g, (window, d) in enumerate(DSA_GROUPS):
            assert window // d == BLOCK and S % (BLOCK * d) == 0 and (B * S) % (DSA_BLOCKS_PER_STEP * BLOCK) == 0
            qg, kg, vg = (a_qkv[t * ng + g].reshape(B * S, DSA_OUT) for t in range(3))
            o, lse = _dsa(qg, kg, vg, S // d // BLOCK)
            oas.append(o.reshape(B, d, S // d, DSA_OUT))
            lses.append(lse.reshape(B, d, S // d, DSA_OUT))
        ob = _mla(bq, bk, bv, B, S, min(MLA_TQ, S), min(MLA_TK, S))
        oc = _sb(cq, ck, cv, B, S, min(SB_TQ, S))

        xt = _merge(xt, p, oas, lses, ob, oc, S, min(256, S))
        xt = _mlp(xt, p['gm'], p['w1'], p['w2'], tm)
    return xt.reshape(B, S, D_MODEL)
```

```python
import functools

import jax
import jax.numpy as jnp
from jax import lax
from jax.experimental import pallas as pl
from jax.experimental.pallas import tpu as pltpu

D_MODEL = 1024
HEAD_DIM = 64
ROPE_THETA = 500000.0
ROT_DIM = HEAD_DIM // 4
NORM_EPS = 1e-6
BLOCK = 128

DSA_GROUPS = ((128, 1), (512, 4), (2048, 16))
DSA_HEADS_PER_GROUP = 4
DSA_W = DSA_HEADS_PER_GROUP * len(DSA_GROUPS) * HEAD_DIM
DSA_OUT = DSA_HEADS_PER_GROUP * HEAD_DIM

MLA_HEADS = 8
MLA_Q_LORA = 256
MLA_KV_LORA = 128
MLA_NOPE = 64
MLA_ROPE = 32
MLA_QK = MLA_NOPE + MLA_ROPE
MLA_V = 64
MLA_OUT = MLA_HEADS * MLA_V
MLA_PAD = 128

SB_HEADS = 8
SB_W = SB_HEADS * HEAD_DIM

N_BRANCH = 3
D_FF = 4 * D_MODEL

LANES = 128
MXU_DIM = 256
VMEM_LIMIT = 56 * 1024 * 1024
TOKEN_TILE = 512
MLP_CHUNK = 512
MLA_TQ, MLA_TK = 512, 512
SB_TQ = 256
SB_PAIRS_PER_STEP = 2
DSA_BLOCKS_PER_STEP = 4

F32 = jnp.float32
BF16 = jnp.bfloat16
NEG = -1e30
LOG2E = 1.4426950408889634
SB_DEAD_LOG2 = -160.0


def _dot(a, b):
    return jnp.dot(a, b, preferred_element_type=F32)


def _dot_nt(a, b):
    return lax.dot_general(a, b, (((1,), (1,)), ((), ())), preferred_element_type=F32)


def _split_dot(s, m):
    hi = s.astype(BF16)
    lo = (s - hi.astype(F32)).astype(BF16)
    return _dot(hi, m) + _dot(lo, m)


def _rms(x, g):
    ms = jnp.mean(x * x, axis=-1, keepdims=True)
    return x * lax.rsqrt(ms + NORM_EPS) * g


def _head_norm(y, m, inv_n, g):
    ss = _split_dot(y * y, m)
    return y * lax.rsqrt(ss * inv_n + NORM_EPS) * g


def _rope(y, c, s1, s2, shift):
    outs = []
    for j in range(y.shape[1] // LANES):
        yj = y[:, j * LANES:(j + 1) * LANES]
        outs.append(yj * c + pltpu.roll(yj, shift, 1) * s1 + pltpu.roll(yj, LANES - shift, 1) * s2)
    return jnp.concatenate(outs, axis=-1)


def _proj_kernel(x_ref, gn_ref, wa_ref, ws_ref, wc_ref, wqb_ref, wk_ref, wv_ref,
                 gaq_ref, gak_ref, gql_ref, gkvl_ref, gbq_ref, gbk_ref,
                 ca_ref, s1a_ref, s2a_ref, cm_ref, s1m_ref, s2m_ref, m64_ref, m128_ref,
                 *rest):
    n_a = 3 * len(DSA_GROUPS)
    a_refs, (bq_ref, bk_ref, bv_ref, cq_ref, ck_ref, cv_ref) = rest[:n_a], rest[n_a:n_a + 6]
    scr_refs = rest[n_a + 6:]
    h = _rms(x_ref[...], gn_ref[...]).astype(BF16)
    ca, s1a, s2a = ca_ref[...], s1a_ref[...], s2a_ref[...]
    cm, s1m, s2m = cm_ref[...], s1m_ref[...], s2m_ref[...]
    m64, m128 = m64_ref[...], m128_ref[...]
    W = MXU_DIM
    tm = x_ref.shape[0]

    jobs = []
    low = {}

    def low_rank(ys):
        low['ql'] = _rms(ys[:, :MLA_Q_LORA], gql_ref[...]).astype(BF16)
        low['kvl'] = _rms(ys[:, MLA_Q_LORA:MLA_Q_LORA + MLA_KV_LORA], gkvl_ref[...]).astype(BF16)
        kr = ys[:, MLA_Q_LORA + MLA_KV_LORA:]
        low['kr2'] = jnp.concatenate([kr, kr], axis=-1)
    jobs.append((lambda: _dot(h, ws_ref[...]), low_rank))

    scr = iter(scr_refs)
    for t, g_ref in enumerate((gaq_ref, gak_ref, None)):
        for g, (_, d) in enumerate(DSA_GROUPS):
            def a_epilogue(y, g_ref=g_ref, g=g, d=d, dst=a_refs[t * len(DSA_GROUPS) + g],
                           buf=next(scr) if d > 1 else None):
                if g_ref is not None:
                    y = _head_norm(y, m64, 1.0 / HEAD_DIM, g_ref[:, g * W:(g + 1) * W])
                    y = _rope(y, ca, s1a, s2a, ROT_DIM // 2)
                if d == 1:
                    dst[0] = y.astype(BF16)
                    return
                for half in range(W // LANES):
                    buf[half] = y[:, half * LANES:(half + 1) * LANES]
                for r in range(d):
                    for half in range(W // LANES):
                        dst[r, :, half * LANES:(half + 1) * LANES] = (
                            buf[half, pl.ds(r, tm // d, stride=d), :].astype(BF16))
            col = t * DSA_W + g * W
            jobs.append((lambda col=col: _dot(h, wa_ref[:, col:col + W]), a_epilogue))

    for n, (dst, scale) in enumerate(((cq_ref, HEAD_DIM ** -0.5 * LOG2E), (ck_ref, None), (cv_ref, None))):
        def c_epilogue(y, dst=dst, scale=scale):
            dst[...] = (y if scale is None else y * scale).astype(BF16)
        jobs.append((lambda n=n: _dot(h, wc_ref[:, n * SB_W:(n + 1) * SB_W]), c_epilogue))

    for c in range(MLA_HEADS * MLA_PAD // W):
        cols = slice(c * W, (c + 1) * W)

        def q_epilogue(y, cols=cols):
            y = _head_norm(y, m128, 1.0 / MLA_QK, gbq_ref[:, cols])
            bq_ref[:, cols] = _rope(y, cm, s1m, s2m, MLA_ROPE // 2).astype(BF16)

        def k_epilogue(y, cols=cols):
            y = _head_norm(y + low['kr2'], m128, 1.0 / MLA_QK, gbk_ref[:, cols])
            bk_ref[:, cols] = _rope(y, cm, s1m, s2m, MLA_ROPE // 2).astype(BF16)

        jobs.append((lambda cols=cols: _dot(low['ql'], wqb_ref[:, cols]), q_epilogue))
        jobs.append((lambda cols=cols: _dot(low['kvl'], wk_ref[:, cols]), k_epilogue))

    def v_epilogue(y):
        bv_ref[...] = y.astype(BF16)
    jobs.append((lambda: _dot(low['kvl'], wv_ref[...]), v_epilogue))

    y = jobs[0][0]()
    for j, (_, epilogue) in enumerate(jobs):
        y_next = jobs[j + 1][0]() if j + 1 < len(jobs) else None
        epilogue(y)
        y = y_next


def _const_spec(shape):
    nd = len(shape)
    return pl.BlockSpec(shape, lambda *_: (0,) * nd, pipeline_mode=pl.Buffered(1))


def _proj(x, p, tabs, S, tm):
    T = x.shape[0]
    ns = S // tm
    row = lambda w: pl.BlockSpec((tm, w), lambda i: (i, 0))
    tab = pl.BlockSpec((tm, LANES), lambda i: (i % ns, 0))
    consts = [p['gn'], p['wa'], p['ws'], p['wc'], p['wqb'], p['wk'], p['wv'],
              p['gaq'], p['gak'], p['gql'], p['gkvl'], p['gbq'], p['gbk']]
    ins = [x] + consts + list(tabs['rope']) + [tabs['m64'], tabs['m128']]
    in_specs = ([row(D_MODEL)] + [_const_spec(c.shape) for c in consts] + [tab] * 6
                + [_const_spec((MXU_DIM, MXU_DIM))] * 2)
    widths = (MLA_HEADS * MLA_PAD, MLA_HEADS * MLA_PAD, MLA_OUT, SB_W, SB_W, SB_W)
    B = T // S
    dils = [d for _ in range(3) for _, d in DSA_GROUPS]
    a_specs = [pl.BlockSpec((None, d, tm // d, DSA_OUT), lambda i: (i // ns, 0, i % ns, 0)) for d in dils]
    a_shapes = [jax.ShapeDtypeStruct((B, d, S // d, DSA_OUT), BF16) for d in dils]
    outs = pl.pallas_call(
        _proj_kernel,
        grid=(T // tm,),
        in_specs=in_specs,
        out_specs=a_specs + [row(w) for w in widths],
        out_shape=a_shapes + [jax.ShapeDtypeStruct((T, w), BF16) for w in widths],
        scratch_shapes=[pltpu.VMEM((DSA_OUT // LANES, tm, LANES), F32) for d in dils if d > 1],
        compiler_params=pltpu.CompilerParams(vmem_limit_bytes=VMEM_LIMIT),
        name="proj",
    )(*ins)
    return outs[:len(dils)], outs[len(dils):]


def _dsa_kernel(q_ref, kp_ref, kc_ref, vp_ref, vc_ref, o_ref, lse_ref, *, nb_seq):
    n = pl.program_id(0)
    nblk = DSA_BLOCKS_PER_STEP
    pens = [jnp.where((nblk * n + blk) % nb_seq == 0, NEG, 0.0).astype(F32) for blk in range(nblk)]
    kj = lax.broadcasted_iota(jnp.int32, (2 * BLOCK, 2 * BLOCK), 0)
    qc = lax.broadcasted_iota(jnp.int32, (2 * BLOCK, 2 * BLOCK), 1)
    qi = jnp.where(qc >= BLOCK, qc - BLOCK, qc)
    band = (kj >= qi) & (kj <= qi + BLOCK)
    lane = lax.broadcasted_iota(jnp.int32, (BLOCK, LANES), 1)
    dim = lax.broadcasted_iota(jnp.int32, (LANES, BLOCK), 0)
    tiles = [(blk, pr) for blk in range(nblk) for pr in range(DSA_OUT // LANES)]

    def operands(blk, pr, ref_prev, ref_cur):
        sl = slice(pr * LANES, (pr + 1) * LANES)
        if blk == 0:
            return jnp.concatenate([ref_prev[:, sl], ref_cur[:BLOCK, sl]], axis=0)
        return ref_cur[(blk - 1) * BLOCK:(blk + 1) * BLOCK, sl]

    s_tiles = []
    for blk, pr in tiles:
        q = q_ref[blk * BLOCK:(blk + 1) * BLOCK, pr * LANES:(pr + 1) * LANES]
        q2 = jnp.concatenate([jnp.where(lane < HEAD_DIM, q, jnp.zeros_like(q)),
                              jnp.where(lane >= HEAD_DIM, q, jnp.zeros_like(q))], axis=0)
        s = _dot_nt(operands(blk, pr, kp_ref, kc_ref), q2)
        s_tiles.append(jnp.where(band, s + jnp.where(kj < BLOCK, pens[blk], 0.0), NEG))
    s = jnp.concatenate(s_tiles, axis=-1)
    m = jnp.max(s, axis=0, keepdims=True)
    p = jnp.exp(s - m)
    den = jnp.sum(p, axis=0, keepdims=True)
    lse = m + jnp.log(den)
    p = p.astype(BF16)
    W = 2 * BLOCK
    for t, (blk, pr) in enumerate(tiles):
        cols = slice(t * W, (t + 1) * W)
        o = lax.dot_general(operands(blk, pr, vp_ref, vc_ref), p[:, cols], (((0,), (0,)), ((), ())),
                            preferred_element_type=F32) / den[:, cols]
        l = jnp.broadcast_to(lse[:, cols], o.shape)
        rows, sl = slice(blk * BLOCK, (blk + 1) * BLOCK), slice(pr * LANES, (pr + 1) * LANES)
        o_ref[rows, sl] = jnp.where(dim < HEAD_DIM, o[:, :BLOCK], o[:, BLOCK:]).T
        lse_ref[rows, sl] = jnp.where(dim < HEAD_DIM, l[:, :BLOCK], l[:, BLOCK:]).T


def _dsa(q, k, v, nb_seq):
    T = q.shape[0]
    nblk = DSA_BLOCKS_PER_STEP
    cur = pl.BlockSpec((nblk * BLOCK, DSA_OUT), lambda n: (n, 0))
    prev = pl.BlockSpec((BLOCK, DSA_OUT), lambda n: (jnp.maximum(nblk * n - 1, 0), 0))
    return pl.pallas_call(
        functools.partial(_dsa_kernel, nb_seq=nb_seq),
        grid=(T // (nblk * BLOCK),),
        in_specs=[cur, prev, cur, prev, cur],
        out_specs=[cur, cur],
        out_shape=[jax.ShapeDtypeStruct((T, DSA_OUT), F32)] * 2,
        name="dsa",
    )(q, k, k, v, v)


def _mla_kernel(q_ref, k_ref, v_ref, o_ref, *, tq, tk):
    i = pl.program_id(2)
    krow = lax.broadcasted_iota(jnp.int32, (tk, 2 * tq), 0)
    qcol = lax.broadcasted_iota(jnp.int32, (tk, 2 * tq), 1)
    qidx = jnp.where(qcol >= tq, qcol - tq, qcol)
    qs = [q_ref[:, hh * MLA_PAD:(hh + 1) * MLA_PAD] for hh in range(2)]
    ndiag = tq // tk

    def scores(kb):
        start = pl.multiple_of(kb * tk, tk)
        return jnp.concatenate([_dot_nt(k_ref[pl.ds(start, tk), hh * MLA_PAD:(hh + 1) * MLA_PAD], qs[hh])
                                for hh in range(2)], axis=-1)

    def absorb(kb, s, carry, diag):
        m, l, acc = carry
        start = pl.multiple_of(kb * tk, tk)
        if diag is not None:
            s = jnp.where(krow + diag * tk <= qidx, s, NEG)
        m_new = jnp.maximum(m, jnp.max(s, axis=0, keepdims=True))
        alpha = jnp.exp2(m - m_new)
        p = jnp.exp2(s - m_new)
        l = l * alpha + jnp.sum(p, axis=0, keepdims=True)
        acc = acc * alpha + lax.dot_general(v_ref[pl.ds(start, tk), :], p.astype(BF16),
                                            (((0,), (0,)), ((), ())), preferred_element_type=F32)
        return m_new, l, acc

    def absorb_all(tiles, carry):
        s = scores(tiles[0][0])
        for t, (kb, diag) in enumerate(tiles):
            s_next = scores(tiles[t + 1][0]) if t + 1 < len(tiles) else None
            carry = absorb(kb, s, carry, diag)
            s = s_next
        return carry

    nfull = i * ndiag
    carry = (jnp.full((1, 2 * tq), NEG, F32), jnp.zeros((1, 2 * tq), F32), jnp.zeros((LANES, 2 * tq), F32))
    carry = lax.fori_loop(0, nfull // 2,
                          lambda n, c: absorb_all([(2 * n, None), (2 * n + 1, None)], c), carry)
    diag_tiles = [(nfull + j, j) for j in range(ndiag)]
    carry = lax.cond(nfull % 2 == 1,
                     lambda c: absorb_all([(nfull - 1, None)] + diag_tiles, c),
                     lambda c: absorb_all(diag_tiles, c), carry)
    _, l, acc = carry
    out = acc / l
    dim = lax.broadcasted_iota(jnp.int32, (LANES, tq), 0)
    o_ref[...] = jnp.where(dim < MLA_V, out[:, :tq], out[:, tq:]).T.astype(BF16)


def _mla(q, k, v, B, S, tq, tk):
    nq = S // tq
    npair = MLA_HEADS // 2
    return pl.pallas_call(
        functools.partial(_mla_kernel, tq=tq, tk=tk),
        grid=(B, npair, nq),
        in_specs=[pl.BlockSpec((tq, 2 * MLA_PAD), lambda b, j, i: (b * nq + i, j)),
                  pl.BlockSpec((S, 2 * MLA_PAD), lambda b, j, i: (b, j)),
                  pl.BlockSpec((S, LANES), lambda b, j, i: (b, j))],
        out_specs=pl.BlockSpec((tq, LANES), lambda b, j, i: (b * nq + i, j)),
        out_shape=jax.ShapeDtypeStruct((B * S, MLA_OUT), BF16),
        compiler_params=pltpu.CompilerParams(vmem_limit_bytes=VMEM_LIMIT),
        name="mla",
    )(q, k, v)


def _sb_kernel(q_ref, k_ref, v_ref, o_ref, *, tq):
    i = pl.program_id(2)
    pairs = [_sb_pair(q_ref, k_ref, v_ref, o_ref, slice(pr * LANES, (pr + 1) * LANES), i, tq)
             for pr in range(q_ref.shape[1] // LANES)]

    def least_decayed(carries):
        top = jnp.max(carries[0][0])
        for after, _ in carries[1:]:
            top = jnp.maximum(top, jnp.max(after))
        return top

    def cond(state):
        n, top, _ = state
        return jnp.logical_and(n < i - 1, top > SB_DEAD_LOG2)

    def body(state):
        n, _, carries = state
        carries = tuple(step(i - 2 - n, carry) for (_, step, _), carry in zip(pairs, carries))
        return n + 1, least_decayed(carries), carries

    carries = tuple(first_step() for first_step, _, _ in pairs)
    _, _, carries = lax.while_loop(cond, body, (jnp.int32(0), least_decayed(carries), carries))
    for (_, _, write), (_, acc) in zip(pairs, carries):
        write(acc)


def _sb_pair(q_ref, k_ref, v_ref, o_ref, cs, i, tq):
    r = lax.broadcasted_iota(jnp.int32, (tq, tq), 0)
    c = lax.broadcasted_iota(jnp.int32, (tq, tq), 1)
    later = jnp.where(c > r, 1.0, 0.0).astype(BF16)
    lane = lax.broadcasted_iota(jnp.int32, (tq, LANES), 1)
    q = q_ref[:, cs]
    q2 = jnp.concatenate([jnp.where(lane < HEAD_DIM, q, jnp.zeros_like(q)),
                          jnp.where(lane >= HEAD_DIM, q, jnp.zeros_like(q))], axis=0)

    def log_terms(start, nkeys):
        z = _dot_nt(k_ref[pl.ds(start, nkeys), cs], q2)
        log_beta = jnp.minimum(z, 0.0) - jnp.log(1.0 + jnp.exp2(-jnp.abs(z))) * LOG2E
        return log_beta, log_beta - z

    def later_sum(lom):
        hi = lom.astype(BF16)
        lo = (lom - hi.astype(F32)).astype(BF16)
        return _dot(later, hi) + _dot(later, lo)

    def weighted_values(start, nkeys, w):
        return lax.dot_general(v_ref[pl.ds(start, nkeys), cs], w.astype(BF16), (((0,), (0,)), ((), ())),
                               preferred_element_type=F32)

    def step(kb, carry):
        after, acc = carry
        start = pl.multiple_of(kb * tq, tq)
        log_beta, lom = log_terms(start, tq)
        w = jnp.exp2(log_beta + later_sum(lom) + after)
        return after + jnp.sum(lom, axis=0, keepdims=True), acc + weighted_values(start, tq, w)

    def first_step():
        start = pl.multiple_of(jnp.maximum(i - 1, 0) * tq, tq)
        kpos = start + lax.broadcasted_iota(jnp.int32, (2 * tq, 2 * tq), 0)
        qcol = lax.broadcasted_iota(jnp.int32, (2 * tq, 2 * tq), 1)
        strict = kpos < i * tq + jnp.where(qcol >= tq, qcol - tq, qcol)
        log_beta, lom = log_terms(start, 2 * tq)
        lom = jnp.where(strict, lom, 0.0)
        sum_hi = jnp.sum(lom[tq:], axis=0, keepdims=True)
        inner = jnp.concatenate([later_sum(lom[:tq]) + sum_hi, later_sum(lom[tq:])], axis=0)
        w = jnp.where(strict, jnp.exp2(log_beta + inner), 0.0)
        return sum_hi + jnp.sum(lom[:tq], axis=0, keepdims=True), weighted_values(start, 2 * tq, w)

    def write(acc):
        dim = lax.broadcasted_iota(jnp.int32, (LANES, tq), 0)
        o_ref[:, cs] = jnp.where(dim < HEAD_DIM, acc[:, :tq], acc[:, tq:]).T.astype(BF16)

    return first_step, step, write


def _sb(q, k, v, B, S, tq):
    nq = S // tq
    w = SB_PAIRS_PER_STEP * LANES
    return pl.pallas_call(
        functools.partial(_sb_kernel, tq=tq),
        grid=(B, SB_W // w, nq),
        in_specs=[pl.BlockSpec((tq, w), lambda b, j, i: (b * nq + i, j)),
                  pl.BlockSpec((S, w), lambda b, j, i: (b, j)),
                  pl.BlockSpec((S, w), lambda b, j, i: (b, j))],
        out_specs=pl.BlockSpec((tq, w), lambda b, j, i: (b * nq + i, j)),
        out_shape=jax.ShapeDtypeStruct((B * S, SB_W), BF16),
        compiler_params=pltpu.CompilerParams(vmem_limit_bytes=VMEM_LIMIT),
        name="sb",
    )(q, k, v)


def _merge_kernel(x_ref, gn_ref, *rest):
    ng = len(DSA_GROUPS)
    oa_refs, lse_refs = rest[:ng], rest[ng:2 * ng]
    ob_ref, oc_ref, wg_ref, wb_ref, wo_ref, out_ref = rest[2 * ng:2 * ng + 6]
    scr = iter(rest[2 * ng + 6:])
    x = x_ref[...]
    tm = x.shape[0]
    proj_b = _dot(ob_ref[...], wb_ref[DSA_OUT:DSA_OUT + MLA_OUT, :])
    proj_c = _dot(oc_ref[...], wb_ref[DSA_OUT + MLA_OUT:, :])
    h = _rms(x, gn_ref[...]).astype(BF16)

    def token_major(ref, d):
        if d == 1:
            return ref[0]
        buf = next(scr)
        nh = DSA_OUT // LANES
        for r in range(d):
            for half in range(nh):
                buf[half, pl.ds(r, tm // d, stride=d), :] = ref[r, :, half * LANES:(half + 1) * LANES]
        return jnp.concatenate([buf[half] for half in range(nh)], axis=-1)

    def gate(b):
        return _dot(h, wg_ref[:, b * D_MODEL:(b + 1) * D_MODEL])

    gate_b = gate(1)
    os_ = [token_major(oa_refs[g], d) for g, (_, d) in enumerate(DSA_GROUPS)]
    l0, l1, l2 = [token_major(lse_refs[g], d) for g, (_, d) in enumerate(DSA_GROUPS)]
    m = jnp.maximum(jnp.maximum(l0, l1), l2)
    e0, e1, e2 = jnp.exp(l0 - m), jnp.exp(l1 - m), jnp.exp(l2 - m)
    oa = ((e0 * os_[0] + e1 * os_[1] + e2 * os_[2]) / (e0 + e1 + e2)).astype(BF16)
    gate_c = gate(2)
    merged = jax.nn.sigmoid(gate_b) * proj_b
    gate_a, proj_a = gate(0), _dot(oa, wb_ref[:DSA_OUT, :])
    merged = merged + jax.nn.sigmoid(gate_c) * proj_c
    merged = merged + jax.nn.sigmoid(gate_a) * proj_a
    out_ref[...] = x + _dot(merged.astype(BF16), wo_ref[...])


def _merge(x, p, oas, lses, ob, oc, S, tm):
    T = x.shape[0]
    ns = S // tm
    row = lambda w: pl.BlockSpec((tm, w), lambda i: (i, 0))
    grp = [pl.BlockSpec((None, d, tm // d, DSA_OUT), lambda i: (i // ns, 0, i % ns, 0)) for _, d in DSA_GROUPS]
    consts = [p['wg'], p['wb'], p['wo']]
    return pl.pallas_call(
        _merge_kernel,
        grid=(T // tm,),
        in_specs=[row(D_MODEL), _const_spec(p['gn'].shape)] + grp + grp + [row(MLA_OUT), row(SB_W)]
                 + [_const_spec(c.shape) for c in consts],
        out_specs=row(D_MODEL),
        out_shape=jax.ShapeDtypeStruct((T, D_MODEL), F32),
        scratch_shapes=[pltpu.VMEM((DSA_OUT // LANES, tm, LANES), F32)
                        for _ in range(2) for _, d in DSA_GROUPS if d > 1],
        compiler_params=pltpu.CompilerParams(vmem_limit_bytes=VMEM_LIMIT),
        name="merge",
    )(x, p['gn'], *oas, *lses, ob, oc, *consts)


def _mlp_kernel(x_ref, g_ref, w1_ref, w2_ref, out_ref, *, chunk):
    x = x_ref[...]
    h = _rms(x, g_ref[...]).astype(BF16)
    acc = x
    for c in range(D_FF // chunk):
        a = jnp.maximum(_dot(h, w1_ref[:, c * chunk:(c + 1) * chunk]), 0.0)
        acc = acc + _dot((a * a).astype(BF16), w2_ref[c * chunk:(c + 1) * chunk, :])
    out_ref[...] = acc


def _mlp(x, g, w1, w2, tm, chunk):
    T = x.shape[0]
    row = pl.BlockSpec((tm, D_MODEL), lambda i: (i, 0))
    return pl.pallas_call(
        functools.partial(_mlp_kernel, chunk=chunk),
        grid=(T // tm,),
        in_specs=[row, _const_spec(g.shape), _const_spec(w1.shape), _const_spec(w2.shape)],
        out_specs=row,
        out_shape=jax.ShapeDtypeStruct((T, D_MODEL), F32),
        compiler_params=pltpu.CompilerParams(vmem_limit_bytes=VMEM_LIMIT),
        name="mlp",
    )(x, g, w1, w2)


def _rope_lane_tables(S, dim, first_lane, period):
    half = dim // 2
    pos = jnp.arange(S, dtype=F32)
    inv_freq = ROPE_THETA ** (-jnp.arange(0, dim, 2, dtype=F32) / dim)
    ang = pos[:, None] * inv_freq[None, :]
    cos, sin = jnp.cos(ang), jnp.sin(ang)
    rel = jnp.arange(LANES) % period - first_lane
    idx = jnp.clip(rel, 0, dim - 1) % half
    in1 = (rel >= 0) & (rel < half)
    in2 = (rel >= half) & (rel < dim)
    c = jnp.where((in1 | in2)[None, :], cos[:, idx], 1.0)
    s1 = jnp.where(in2[None, :], sin[:, idx], 0.0)
    s2 = jnp.where(in1[None, :], -sin[:, idx], 0.0)
    return c, s1, s2


def _block_diag_ones(block):
    r = jnp.arange(MXU_DIM) // block
    return (r[:, None] == r[None, :]).astype(BF16)


def _prep_params(attn_norm, w_in, a_q_norm, a_k_norm, b_q_a_norm, w_q_b, b_kv_a_norm, w_kv_b,
                 b_q_norm, b_k_norm, w_branch, w_out, mlp_norm, w_ff1, w_ff2):
    L = w_in.shape[0]
    o_ql = 3 * DSA_W
    o_kr = o_ql + MLA_Q_LORA + MLA_KV_LORA
    o_c = o_kr + MLA_ROPE
    o_g = o_c + 3 * SB_W
    zeros = lambda n: jnp.zeros((L, D_MODEL, n), F32)
    ws = jnp.concatenate([w_in[:, :, o_ql:o_kr], zeros(MLA_NOPE), w_in[:, :, o_kr:o_c],
                          zeros(MLA_PAD - MLA_QK)], axis=-1)
    pad_head = lambda w: jnp.pad(w, ((0, 0), (0, 0), (0, 0), (0, MLA_PAD - w.shape[-1]))).reshape(
        L, w.shape[1], MLA_HEADS * MLA_PAD)
    pad_gain = lambda g: jnp.tile(jnp.pad(g, ((0, 0), (0, MLA_PAD - MLA_QK))), (1, MLA_HEADS))[:, None, :]
    n_heads_a = DSA_W // HEAD_DIM
    return {
        'gn': attn_norm[:, None, :],
        'wa': w_in[:, :, :o_ql].astype(BF16),
        'ws': ws.astype(BF16),
        'wc': w_in[:, :, o_c:o_g].astype(BF16),
        'wg': w_in[:, :, o_g:].astype(BF16),
        'wqb': pad_head(w_q_b).astype(BF16),
        'wk': pad_head(w_kv_b[..., :MLA_NOPE]).astype(BF16),
        'wv': w_kv_b[..., MLA_NOPE:].reshape(L, MLA_KV_LORA, MLA_OUT).astype(BF16),
        'gaq': jnp.tile(a_q_norm * HEAD_DIM ** -0.5, (1, n_heads_a))[:, None, :],
        'gak': jnp.tile(a_k_norm, (1, n_heads_a))[:, None, :],
        'gql': b_q_a_norm[:, None, :],
        'gkvl': b_kv_a_norm[:, None, :],
        'gbq': pad_gain(b_q_norm * (MLA_QK ** -0.5 * LOG2E)),
        'gbk': pad_gain(b_k_norm),
        'wb': w_branch.astype(BF16),
        'wo': w_out.astype(BF16),
        'gm': mlp_norm[:, None, :],
        'w1': w_ff1.astype(BF16),
        'w2': w_ff2.astype(BF16),
    }


def kernel(x, attn_norm, w_in, a_q_norm, a_k_norm, b_q_a_norm, w_q_b, b_kv_a_norm, w_kv_b, b_q_norm,
           b_k_norm, w_branch, w_out, mlp_norm, w_ff1, w_ff2):
    B, S, _ = x.shape
    depth = w_in.shape[0]
    params = _prep_params(attn_norm, w_in, a_q_norm, a_k_norm, b_q_a_norm, w_q_b, b_kv_a_norm, w_kv_b,
                          b_q_norm, b_k_norm, w_branch, w_out, mlp_norm, w_ff1, w_ff2)
    tabs = {
        'rope': _rope_lane_tables(S, ROT_DIM, 0, HEAD_DIM) + _rope_lane_tables(S, MLA_ROPE, MLA_NOPE, MLA_PAD),
        'm64': _block_diag_ones(HEAD_DIM),
        'm128': _block_diag_ones(MLA_PAD),
    }
    tm = min(TOKEN_TILE, S)
    xt = x.reshape(B * S, D_MODEL)
    for l in range(depth):
        p = {k: v[l] for k, v in params.items()}
        a_qkv, (bq, bk, bv, cq, ck, cv) = _proj(xt, p, tabs, S, tm)

        oas, lses = [], []
        ng = len(DSA_GROUPS)
        for g, (window, d) in enumerate(DSA_GROUPS):
            assert window // d == BLOCK and S % (BLOCK * d) == 0 and (B * S) % (DSA_BLOCKS_PER_STEP * BLOCK) == 0
            qg, kg, vg = (a_qkv[t * ng + g].reshape(B * S, DSA_OUT) for t in range(3))
            o, lse = _dsa(qg, kg, vg, S // d // BLOCK)
            oas.append(o.reshape(B, d, S // d, DSA_OUT))
            lses.append(lse.reshape(B, d, S // d, DSA_OUT))
        ob = _mla(bq, bk, bv, B, S, min(MLA_TQ, S), min(MLA_TK, S))
        oc = _sb(cq, ck, cv, B, S, min(SB_TQ, S))

        xt = _merge(xt, p, oas, lses, ob, oc, S, tm)
        xt = _mlp(xt, p['gm'], p['w1'], p['w2'], tm, MLP_CHUNK)
    return xt.reshape(B, S, D_MODEL)
```
